```python
import math
import jax, jax.numpy as jnp
from jax import lax
import numpy as np

D_MODEL = 1024
BATCH = 32
SEQ = 2048
DEPTH = 1
DEC_BATCH = 128
DEC_SEQ = 4
PAST_LEN = 8192
PAGE_SIZE = 128

N_HEADS = 8
D_QK = 64
D_V = 2 * D_QK
QK_W = N_HEADS * 2 * D_QK
ATTN_W = N_HEADS * D_V
D_CONV = D_MODEL
CONV_W = 3
D_FF = 4 * D_MODEL
ROPE_THETA = 10000.0
EPS = 1e-6
Q_BLOCK = 128
SCALE = 1.0 / math.sqrt(D_QK)
IN_W = 2 * QK_W + ATTN_W + 3 * D_CONV + 2 * D_MODEL
SPLITS = [QK_W, 2 * QK_W, 2 * QK_W + ATTN_W, 2 * QK_W + ATTN_W + D_CONV,
          2 * QK_W + ATTN_W + 2 * D_CONV, 2 * QK_W + ATTN_W + 3 * D_CONV,
          2 * QK_W + ATTN_W + 3 * D_CONV + D_MODEL]

kernel_name = "hybrid_diffattn_shortconv_gated_decode_step"

F32 = jnp.float32


def _rms_norm(x, g):
    xf = x.astype(F32)
    y = xf * lax.rsqrt(jnp.mean(xf * xf, axis=-1, keepdims=True) + EPS)
    return (y * g.astype(F32)).astype(x.dtype)


def _rope(x, pos):
    half = D_QK // 2
    inv = ROPE_THETA ** (-jnp.arange(0, D_QK, 2, dtype=F32) / D_QK)
    ang = pos.astype(F32)[:, None] * inv[None, :]
    cos = jnp.cos(ang)[None, :, None, None, :]
    sin = jnp.sin(ang)[None, :, None, None, :]
    xf = x.astype(F32)
    x1, x2 = xf[..., :half], xf[..., half:]
    out = jnp.concatenate([x1 * cos - x2 * sin, x2 * cos + x1 * sin], axis=-1)
    return out.astype(x.dtype)


def _lambda(lq1, lk1, lq2, lk2, lam_init):
    return (jnp.exp(jnp.sum(lq1.astype(F32) * lk1.astype(F32)))
            - jnp.exp(jnp.sum(lq2.astype(F32) * lk2.astype(F32))) + lam_init)


def _mixer_inputs(x, pos, g_norm, w_in, g_q, g_k):
    B, T = x.shape[0], x.shape[1]
    xn = _rms_norm(x, g_norm)
    z = xn @ w_in
    q, k, v, u_c, u_b, u_x, z_ga, z_gc = jnp.split(z, SPLITS, axis=-1)
    q = _rope(_rms_norm(q.reshape(B, T, N_HEADS, 2, D_QK), g_q), pos)
    k = _rope(_rms_norm(k.reshape(B, T, N_HEADS, 2, D_QK), g_k), pos)
    v = v.reshape(B, T, N_HEADS, D_V)
    u = u_c * u_x
    return q, k, v, u, u_b, z_ga, z_gc


def _prompt_diff_attn(q, k, v, lam):
    B, T = q.shape[0], q.shape[1]
    nq = T // Q_BLOCK
    qb = jnp.moveaxis(q.reshape(B, nq, Q_BLOCK, N_HEADS, 2, D_QK), 1, 0)
    kpos = jnp.arange(T)

    def block(args):
        qi, i = args
        qpos = i * Q_BLOCK + jnp.arange(Q_BLOCK)
        mask = qpos[:, None] >= kpos[None, :]
        s = jnp.einsum('bqhmd,bkhmd->bhmqk', qi, k, preferred_element_type=F32) * SCALE
        s = jnp.where(mask, s, -jnp.inf)
        p = jax.nn.softmax(s, axis=-1)
        a = p[:, :, 0] - lam * p[:, :, 1]
        return jnp.einsum('bhqk,bkhe->bqhe', a.astype(v.dtype), v,
                          preferred_element_type=F32).astype(v.dtype)

    o = lax.map(block, (qb, jnp.arange(nq)))
    return jnp.moveaxis(o, 0, 1).reshape(B, T, N_HEADS, D_V)


def _online_update(carry, s, vb):
    m, l, acc = carry
    m_new = jnp.maximum(m, jnp.max(s, axis=-1))
    corr = jnp.exp(m - m_new)
    p = jnp.exp(s - m_new[..., None])
    l = l * corr + jnp.sum(p, axis=-1)
    acc = acc * corr[..., None] + jnp.einsum('bhmqk,bkhe->bhmqe', p.astype(vb.dtype), vb,
                                             preferred_element_type=F32)
    return (m_new, l, acc)


def _sample_diff_attn(q, k_new, v_new, cache_k, cache_v, layer, page_table, lam):
    B, T = q.shape[0], q.shape[1]

    def page_step(carry, phys):
        kp = cache_k[layer, phys].reshape(B, PAGE_SIZE, N_HEADS, 2, D_QK)
        vp = cache_v[layer, phys]
        s = jnp.einsum('bqhmd,bkhmd->bhmqk', q, kp, preferred_element_type=F32) * SCALE
        return _online_update(carry, s, vp), None

    init = (jnp.full((B, N_HEADS, 2, T), -jnp.inf, F32),
            jnp.zeros((B, N_HEADS, 2, T), F32),
            jnp.zeros((B, N_HEADS, 2, T, D_V), F32))
    carry, _ = lax.scan(page_step, init, page_table.T)
    s = jnp.einsum('bqhmd,bkhmd->bhmqk', q, k_new, preferred_element_type=F32) * SCALE
    causal = jnp.tril(jnp.ones((T, T), dtype=bool))
    s = jnp.where(causal, s, -jnp.inf)
    _, l, acc = _online_update(carry, s, v_new)
    o = acc / l[..., None]
    o = o[:, :, 0] - lam * o[:, :, 1]
    return jnp.transpose(o, (0, 2, 1, 3)).astype(q.dtype)


def _short_conv(u_ext, w):
    T = u_ext.shape[1] - (CONV_W - 1)
    y = w[0] * u_ext[:, 0:T]
    for j in range(1, CONV_W):
        y = y + w[j] * u_ext[:, j:j + T]
    return y


def _mixer_out(x, attn_o, conv_y, u_b, z_ga, z_gc, lam_init, g_sub, w_attn_out, w_conv_out, w_o):
    B, T = x.shape[0], x.shape[1]
    a = (_rms_norm(attn_o, g_sub) * (1.0 - lam_init)).reshape(B, T, ATTN_W) @ w_attn_out
    c = (u_b * conv_y) @ w_conv_out
    h = jax.nn.sigmoid(z_ga) * a + jax.nn.sigmoid(z_gc) * c
    return x + h @ w_o


def _mlp(x, g, w_up, w_down):
    h = _rms_norm(x, g) @ w_up
    return x + jnp.square(jax.nn.relu(h)) @ w_down


def setup_inputs(seed: int = 0) -> dict:
    key = jax.random.key(seed)
    ks = jax.random.split(key, 24)
    n_pages = PAST_LEN // PAGE_SIZE
    n_used = DEC_BATCH * n_pages
    n_pool = (5 * n_used + 3) // 4
    nrm = jax.random.normal
    perm = jax.random.permutation(ks[0], n_pool)
    page_table = perm[:n_used].reshape(DEC_BATCH, n_pages).astype(jnp.int32)
    gain = lambda k, n: 1.0 + 0.01 * nrm(k, (DEPTH, n), F32)
    return {
        "x_prompt": nrm(ks[1], (BATCH, SEQ, D_MODEL), F32),
        "x_sample": nrm(ks[2], (DEC_BATCH, DEC_SEQ, D_MODEL), F32),
        "cache_k": nrm(ks[3], (DEPTH, n_pool, PAGE_SIZE, N_HEADS, 2 * D_QK), F32),
        "cache_v": nrm(ks[4], (DEPTH, n_pool, PAGE_SIZE, N_HEADS, D_V), F32),
        "state_conv": nrm(ks[5], (DEPTH, DEC_BATCH, CONV_W - 1, D_CONV), F32),
        "page_table": page_table,
        "g_attn_norm": gain(ks[6], D_MODEL),
        "w_in": nrm(ks[7], (DEPTH, D_MODEL, IN_W), F32) * D_MODEL ** -0.5,
        "g_q": gain(ks[8], D_QK),
        "g_k": gain(ks[9], D_QK),
        "lambda_q1": 0.1 * nrm(ks[10], (DEPTH, D_QK), F32),
        "lambda_k1": 0.1 * nrm(ks[11], (DEPTH, D_QK), F32),
        "lambda_q2": 0.1 * nrm(ks[12], (DEPTH, D_QK), F32),
        "lambda_k2": 0.1 * nrm(ks[13], (DEPTH, D_QK), F32),
        "g_sub": gain(ks[14], D_V),
        "w_attn_out": nrm(ks[15], (DEPTH, ATTN_W, D_MODEL), F32) * ATTN_W ** -0.5,
        "conv_w": nrm(ks[16], (DEPTH, CONV_W, D_CONV), F32) * CONV_W ** -0.5,
        "w_conv_out": nrm(ks[17], (DEPTH, D_CONV, D_MODEL), F32) * D_CONV ** -0.5,
        "w_o": nrm(ks[18], (DEPTH, D_MODEL, D_MODEL), F32) * D_MODEL ** -0.5,
        "g_mlp_norm": gain(ks[19], D_MODEL),
        "w_up": nrm(ks[20], (DEPTH, D_MODEL, D_FF), F32) * D_MODEL ** -0.5,
        "w_down": nrm(ks[21], (DEPTH, D_FF, D_MODEL), F32) * D_FF ** -0.5,
    }


def reference(x_prompt, x_sample, cache_k, cache_v, state_conv, page_table,
              g_attn_norm, w_in, g_q, g_k, lambda_q1, lambda_k1, lambda_q2, lambda_k2,
              g_sub, w_attn_out, conv_w, w_conv_out, w_o, g_mlp_norm, w_up, w_down):
    Bp, Tp = x_prompt.shape[0], x_prompt.shape[1]
    Bs, Ts = x_sample.shape[0], x_sample.shape[1]
    past = page_table.shape[1] * PAGE_SIZE
    pos_p = jnp.arange(Tp)
    pos_s = past + jnp.arange(Ts)
    yp, ys = x_prompt, x_sample
    nk_p, nv_p, nc_p, nk_s, nv_s, nc_s = [], [], [], [], [], []
    for l in range(DEPTH):
        lam_init = 0.8 - 0.6 * math.exp(-0.3 * l)
        lam = _lambda(lambda_q1[l], lambda_k1[l], lambda_q2[l], lambda_k2[l], lam_init)

        q, k, v, u, u_b, z_ga, z_gc = _mixer_inputs(yp, pos_p, g_attn_norm[l], w_in[l], g_q[l], g_k[l])
        o = _prompt_diff_attn(q, k, v, lam)
        u_ext = jnp.pad(u, ((0, 0), (CONV_W - 1, 0), (0, 0)))
        cy = _short_conv(u_ext, conv_w[l])
        nk_p.append(k.reshape(Bp, Tp, N_HEADS, 2 * D_QK))
        nv_p.append(v)
        nc_p.append(u_ext[:, -(CONV_W - 1):])
        yp = _mixer_out(yp, o, cy, u_b, z_ga, z_gc, lam_init, g_sub[l], w_attn_out[l], w_conv_out[l], w_o[l])
        yp = _mlp(yp, g_mlp_norm[l], w_up[l], w_down[l])

        q, k, v, u, u_b, z_ga, z_gc = _mixer_inputs(ys, pos_s, g_attn_norm[l], w_in[l], g_q[l], g_k[l])
        o = _sample_diff_attn(q, k, v, cache_k, cache_v, l, page_table, lam)
        u_ext = jnp.concatenate([state_conv[l].astype(u.dtype), u], axis=1)
        cy = _short_conv(u_ext, conv_w[l])
        nk_s.append(k.reshape(Bs, Ts, N_HEADS, 2 * D_QK))
        nv_s.append(v)
        nc_s.append(u_ext[:, -(CONV_W - 1):])
        ys = _mixer_out(ys, o, cy, u_b, z_ga, z_gc, lam_init, g_sub[l], w_attn_out[l], w_conv_out[l], w_o[l])
        ys = _mlp(ys, g_mlp_norm[l], w_up[l], w_down[l])

    return (yp, ys, jnp.stack(nk_p), jnp.stack(nv_p), jnp.stack(nc_p),
            jnp.stack(nk_s), jnp.stack(nv_s), jnp.stack(nc_s))
```

```python
import functools
import math

import jax
import jax.numpy as jnp
from jax import lax
from jax.experimental import pallas as pl
from jax.experimental.pallas import tpu as pltpu

F32 = jnp.float32
BF16 = jnp.bfloat16

N_HEADS = 8
D_QK = 64
D_V = 2 * D_QK
HEAD_W = 2 * D_QK
CONV_W = 3
ROPE_THETA = 10000.0
EPS = 1e-6
SCALE = 1.0 / math.sqrt(D_QK)
N_IN_CHUNKS = 8
NEW_KEY_PAD = 16

V7X_VMEM_LIMIT = 56 * 1024 * 1024
GROUP_TILE = 256

NT_DIMS = (((1,), (1,)), ((), ()))
TN_DIMS = (((0,), (0,)), ((), ()))


def _dot(a, b):
    return jnp.dot(a, b, preferred_element_type=F32)


def _rms_rows(x, g_row):
    return x * lax.rsqrt(jnp.mean(x * x, axis=-1, keepdims=True) + EPS) * g_row


def _lambda(lq1_ref, lk1_ref, lq2_ref, lk2_ref, lam_init):
    s1 = jnp.sum(lq1_ref[...] * lk1_ref[...], axis=-1, keepdims=True)
    s2 = jnp.sum(lq2_ref[...] * lk2_ref[...], axis=-1, keepdims=True)
    return jnp.exp(s1) - jnp.exp(s2) + lam_init


def _qk_norm_rope(z, g_row, gmat, cos, sin_signed, upper_half):
    zz = (z * z).astype(BF16)
    n_tiles = z.shape[1] // GROUP_TILE
    ss = jnp.concatenate(
        [_dot(zz[:, c * GROUP_TILE:(c + 1) * GROUP_TILE], gmat) for c in range(n_tiles)], axis=1)
    y = z * lax.rsqrt(ss * (1.0 / D_QK) + EPS) * g_row
    outs = []
    for h in range(N_HEADS):
        ys = y[:, h * HEAD_W:(h + 1) * HEAD_W]
        partner = jnp.where(upper_half, pltpu.roll(ys, D_QK // 2, 1),
                            pltpu.roll(ys, HEAD_W - D_QK // 2, 1))
        outs.append(ys * cos + partner * sin_signed)
    return jnp.concatenate(outs, axis=1)


def _inproj_common(x, gn_ref, w_ref, gq_ref, gk_ref, gmat_ref, cos_ref, sin_ref):
    rows, d = x.shape
    xn = _rms_rows(x, gn_ref[...]).astype(BF16)
    zc = lambda c: _dot(xn, w_ref[:, c * d:(c + 1) * d])
    cos, sin_signed = cos_ref[...], sin_ref[...]
    upper_half = (lax.broadcasted_iota(jnp.int32, (rows, HEAD_W), 1) & (D_QK // 2)) != 0
    gmat = gmat_ref[...]
    q = _qk_norm_rope(zc(0), gq_ref[...], gmat, cos, sin_signed, upper_half)
    k = _qk_norm_rope(zc(1), gk_ref[...], gmat, cos, sin_signed, upper_half)
    v = zc(2)
    u = zc(3) * zc(5)
    u_b = zc(4)
    sga = jax.nn.sigmoid(zc(6))
    sgc = jax.nn.sigmoid(zc(7))
    return q, k, v, u, u_b, sga, sgc


def _prompt_inproj_kernel(x_ref, gn_ref, w_ref, gq_ref, gk_ref, gmat_ref, cos_ref, sin_ref, cw_ref,
                          kf_ref, vf_ref, qb_ref, kb_ref, vb_ref, cin_ref, sga_ref, sgc_ref, cst_ref,
                          carry_ref):
    i = pl.program_id(1)
    x = x_ref[0]
    tm = x.shape[0]
    q, k, v, u, u_b, sga, sgc = _inproj_common(x, gn_ref, w_ref, gq_ref, gk_ref, gmat_ref,
                                               cos_ref, sin_ref)
    qb_ref[0] = (q * SCALE).astype(BF16)
    kf_ref[0] = k
    kb_ref[0] = k.astype(BF16)
    vf_ref[0] = v
    vb_ref[0] = v.astype(BF16)

    @pl.when(i == 0)
    def _():
        carry_ref[...] = jnp.zeros_like(carry_ref)

    prev = carry_ref[...]
    c0, c1 = prev[6:7], prev[7:8]
    rows = lax.broadcasted_iota(jnp.int32, (tm, 1), 0)
    u1 = jnp.where(rows == 0, c1, pltpu.roll(u, 1, 0))
    u2 = jnp.where(rows == 0, c0, jnp.where(rows == 1, c1, pltpu.roll(u, 2, 0)))
    cw = cw_ref[...]
    cy = cw[0:1] * u2 + cw[1:2] * u1 + cw[2:3] * u
    carry_ref[...] = u[tm - 8:tm]
    cst_ref[0] = u[tm - (CONV_W - 1):tm]
    cin_ref[0] = (u_b * cy).astype(BF16)
    sga_ref[0] = sga.astype(BF16)
    sgc_ref[0] = sgc.astype(BF16)


def _sample_inproj_kernel(x_ref, gn_ref, w_ref, gq_ref, gk_ref, gmat_ref, cos_ref, sin_ref, cw_ref,
                          st_ref, qf_ref, kf_ref, vf_ref, cin_ref, sga_ref, sgc_ref, cst_ref, *, nb):
    x = x_ref[...]
    n = x.shape[0]
    q, k, v, u, u_b, sga, sgc = _inproj_common(x, gn_ref, w_ref, gq_ref, gk_ref, gmat_ref,
                                               cos_ref, sin_ref)
    qf_ref[...] = q
    kf_ref[...] = k
    vf_ref[...] = v
    ue = jnp.concatenate([st_ref[...], u], axis=0)
    cw = cw_ref[...]
    cy = cw[0:1] * ue[0:n] + cw[1:2] * ue[nb:nb + n] + cw[2:3] * ue[2 * nb:2 * nb + n]
    cst_ref[...] = ue[n:n + (CONV_W - 1) * nb]
    cin_ref[...] = (u_b * cy).astype(BF16)
    sga_ref[...] = sga.astype(BF16)
    sgc_ref[...] = sgc.astype(BF16)


def _prompt_attn_kernel(q_ref, k_ref, v_ref, lq1_ref, lk1_ref, lq2_ref, lk2_ref, gsub_ref,
                        o_ref, acc_ref, m_ref, l_ref, *, tq, lam_init):
    qi = pl.program_id(1)
    lam = _lambda(lq1_ref, lk1_ref, lq2_ref, lk2_ref, lam_init)
    lane = lax.broadcasted_iota(jnp.int32, (tq, HEAD_W), 1)
    key_row = lax.broadcasted_iota(jnp.int32, (tq, 2 * tq), 0)
    q_col = lax.broadcasted_iota(jnp.int32, (tq, 2 * tq), 1)
    causal = key_row <= jnp.where(q_col >= tq, q_col - tq, q_col)

    for h in range(N_HEADS):
        hs = slice(h * HEAD_W, (h + 1) * HEAD_W)
        qh = q_ref[0, :, hs]
        zero = jnp.zeros_like(qh)
        qs = jnp.concatenate([jnp.where(lane < D_QK, qh, zero), jnp.where(lane >= D_QK, qh, zero)],
                             axis=0)
        m_ref[...] = jnp.full_like(m_ref, -jnp.inf)
        l_ref[...] = jnp.zeros_like(l_ref)
        acc_ref[...] = jnp.zeros_like(acc_ref)

        def step(kt, masked):
            ks = pl.ds(pl.multiple_of(kt * tq, tq), tq)
            s = lax.dot_general(k_ref[0, ks, hs], qs, NT_DIMS, preferred_element_type=F32)
            if masked:
                s = jnp.where(causal, s, -jnp.inf)
            m_prev = m_ref[...]
            m_new = jnp.maximum(m_prev, jnp.max(s, axis=0, keepdims=True))
            corr = jnp.exp(m_prev - m_new)
            p = jnp.exp(s - m_new)
            l_ref[...] = l_ref[...] * corr + jnp.sum(p, axis=0, keepdims=True)
            pv = lax.dot_general(v_ref[0, ks, hs], p.astype(BF16), TN_DIMS,
                                 preferred_element_type=F32)
            acc_ref[...] = acc_ref[...] * corr + pv
            m_ref[...] = m_new

        def body(kt, carry):
            step(kt, False)
            return carry

        lax.fori_loop(0, qi, body, 0)
        step(qi, True)

        o = acc_ref[...] * (1.0 / l_ref[...])
        od = o[:, :tq] - lam * o[:, tq:]
        ms = jnp.mean(od * od, axis=0, keepdims=True)
        on = od * lax.rsqrt(ms + EPS) * gsub_ref[...] * (1.0 - lam_init)
        o_ref[0, :, hs] = on.T.astype(BF16)


def _decode_attn_kernel(pt_ref, qz_ref, kn_ref, vn_ref, bias_ref, biasn_ref,
                        lq1_ref, lk1_ref, lq2_ref, lk2_ref, gsub_ref, *rest, n_pages_step, lam_init):
    k_refs = rest[:n_pages_step]
    v_refs = rest[n_pages_step:2 * n_pages_step]
    o_ref, m_ref, l_ref, acc_ref = rest[2 * n_pages_step:]
    c = pl.program_id(1)
    n_rows = qz_ref.shape[1]

    @pl.when(c == 0)
    def _():
        m_ref[...] = jnp.full_like(m_ref, -jnp.inf)
        l_ref[...] = jnp.zeros_like(l_ref)
        acc_ref[...] = jnp.zeros_like(acc_ref)

    qz = (qz_ref[0] * SCALE).astype(BF16)

    def scores(k3, bias):
        kp = k3.reshape(k3.shape[0] * k3.shape[1], k3.shape[2]).astype(BF16)
        return lax.dot_general(qz, kp, NT_DIMS, preferred_element_type=F32) + bias

    def update(s_list, v_list):
        m_prev = m_ref[...]
        m_new = m_prev
        for s in s_list:
            m_new = jnp.maximum(m_new, jnp.max(s, axis=1, keepdims=True))
        corr = jnp.exp(m_prev - m_new)
        l_new = l_ref[...] * corr
        acc = acc_ref[...] * corr
        for s, v3 in zip(s_list, v_list):
            p = jnp.exp(s - m_new)
            l_new = l_new + jnp.sum(p, axis=1, keepdims=True)
            vp = v3.reshape(v3.shape[0] * v3.shape[1], v3.shape[2]).astype(BF16)
            acc = acc + _dot(p.astype(BF16), vp)
        m_ref[...] = m_new
        l_ref[...] = l_new
        acc_ref[...] = acc

    bias = bias_ref[...]
    update([scores(k_refs[j][0, 0], bias) for j in range(n_pages_step)],
           [v_refs[j][0, 0] for j in range(n_pages_step)])

    @pl.when(c == pl.num_programs(1) - 1)
    def _():
        update([scores(kn_ref[0], biasn_ref[...])], [vn_ref[0]])
        lam = _lambda(lq1_ref, lk1_ref, lq2_ref, lk2_ref, lam_init)
        o = acc_ref[...] * (1.0 / l_ref[...])
        od = o[:n_rows // 2] - lam * o[n_rows // 2:]
        o_ref[0] = _rms_rows(od, gsub_ref[...]) * (1.0 - lam_init)


def _out_mlp_kernel(an_ref, cin_ref, sga_ref, sgc_ref, x_ref, wa_ref, wc_ref, wo_ref, gm_ref,
                    wup_ref, wdn_ref, y_ref):
    a = _dot(an_ref[...], wa_ref[...])
    cv = _dot(cin_ref[...], wc_ref[...])
    h = sga_ref[...].astype(F32) * a + sgc_ref[...].astype(F32) * cv
    y = x_ref[...] + _dot(h.astype(BF16), wo_ref[...])
    xn = _rms_rows(y, gm_ref[...]).astype(BF16)
    d = y.shape[1]
    out = y
    for c in range(wup_ref.shape[1] // d):
        hc = _dot(xn, wup_ref[:, c * d:(c + 1) * d])
        hc = jnp.square(jnp.maximum(hc, 0.0)).astype(BF16)
        out = out + _dot(hc, wdn_ref[c * d:(c + 1) * d, :])
    y_ref[...] = out


def _resident(shape):
    nd = len(shape)
    return pl.BlockSpec(shape, lambda *_: (0,) * nd, pipeline_mode=pl.Buffered(1))


def _rope_tables(pos):
    inv = ROPE_THETA ** (-jnp.arange(0, D_QK, 2, dtype=F32) / D_QK)
    ang = pos.astype(F32)[:, None] * inv[None, :]
    cos, sin = jnp.cos(ang), jnp.sin(ang)
    return jnp.tile(cos, (1, 4)), jnp.tile(jnp.concatenate([-sin, sin], axis=1), (1, 2))


def _group_sum_matrix():
    r = jnp.arange(GROUP_TILE) // D_QK
    return (r[:, None] == r[None, :]).astype(BF16)


def _row_tile(n, target):
    t = min(n, target)
    while n % t:
        t //= 2
    return t


def _prompt_inproj(x, gn, w_in, gq, gk, gmat, cos, sin_signed, conv_w, tm):
    b, t, d = x.shape
    tile = lambda dt: jax.ShapeDtypeStruct((b, t, d), dt)
    row_spec = pl.BlockSpec((1, tm, d), lambda bi, i: (bi, i, 0))
    tab_spec = pl.BlockSpec((tm, HEAD_W), lambda bi, i: (i, 0))
    return pl.pallas_call(
        _prompt_inproj_kernel,
        grid=(b, t // tm),
        in_specs=[row_spec, _resident(gn.shape), _resident(w_in.shape), _resident(gq.shape),
                  _resident(gk.shape), _resident(gmat.shape), tab_spec, tab_spec,
                  _resident(conv_w.shape)],
        out_specs=[row_spec] * 8 + [pl.BlockSpec((1, CONV_W - 1, d), lambda bi, i: (bi, 0, 0))],
        out_shape=[tile(F32), tile(F32), tile(BF16), tile(BF16), tile(BF16), tile(BF16), tile(BF16),
                   tile(BF16), jax.ShapeDtypeStruct((b, CONV_W - 1, d), F32)],
        scratch_shapes=[pltpu.VMEM((8, d), F32)],
        compiler_params=pltpu.CompilerParams(dimension_semantics=("arbitrary", "arbitrary"),
                                             vmem_limit_bytes=V7X_VMEM_LIMIT),
        name="prompt_inproj",
    )(x, gn, w_in, gq, gk, gmat, cos, sin_signed, conv_w)


def _sample_inproj(x, gn, w_in, gq, gk, gmat, cos, sin_signed, conv_w, state, nb):
    n, d = x.shape
    full = lambda dt: jax.ShapeDtypeStruct((n, d), dt)
    args = (x, gn, w_in, gq, gk, gmat, cos, sin_signed, conv_w, state)
    return pl.pallas_call(
        functools.partial(_sample_inproj_kernel, nb=nb),
        out_shape=[full(F32), full(F32), full(F32), full(BF16), full(BF16), full(BF16),
                   jax.ShapeDtypeStruct(((CONV_W - 1) * nb, d), F32)],
        compiler_params=pltpu.CompilerParams(vmem_limit_bytes=V7X_VMEM_LIMIT),
        name="sample_inproj",
    )(*args)


def _prompt_attn(qb, kb, vb, lams, gsub_col, lam_init, tq):
    b, t, d = qb.shape
    q_spec = pl.BlockSpec((1, tq, d), lambda bi, qi: (bi, qi, 0))
    kv_spec = pl.BlockSpec((1, t, d), lambda bi, qi: (bi, 0, 0))
    small = [_resident(a.shape) for a in lams] + [_resident(gsub_col.shape)]
    return pl.pallas_call(
        functools.partial(_prompt_attn_kernel, tq=tq, lam_init=lam_init),
        grid=(b, t // tq),
        in_specs=[q_spec, kv_spec, kv_spec] + small,
        out_specs=q_spec,
        out_shape=jax.ShapeDtypeStruct((b, t, d), BF16),
        scratch_shapes=[pltpu.VMEM((D_V, 2 * tq), F32), pltpu.VMEM((1, 2 * tq), F32),
                        pltpu.VMEM((1, 2 * tq), F32)],
        compiler_params=pltpu.CompilerParams(dimension_semantics=("arbitrary", "arbitrary"),
                                             vmem_limit_bytes=V7X_VMEM_LIMIT),
        name="prompt_attn",
    )(qb, kb, vb, *lams, gsub_col)


def _decode_attn(page_table, qz, kn, vn, cache_k, cache_v, layer, lams, gsub_row, lam_init,
                 n_pages_step):
    nb, n_rows, _ = qz.shape
    n_pages = page_table.shape[1]
    page_size, n_heads, hw = cache_k.shape[2:]
    pt_flat = page_table.reshape(-1)
    row_head = (jnp.arange(n_rows) % (n_rows // 2)) // (n_rows // 2 // n_heads)
    row_tok = jnp.arange(n_rows) % (n_rows // 2 // n_heads)
    col_head = jnp.arange(page_size * n_heads) % n_heads
    neg = jnp.asarray(-jnp.inf, F32)
    bias = jnp.where(row_head[:, None] == col_head[None, :], 0.0, neg).astype(F32)
    coln = jnp.arange(NEW_KEY_PAD * n_heads)
    biasn = jnp.where((row_head[:, None] == (coln % n_heads)[None, :])
                      & ((coln // n_heads)[None, :] <= row_tok[:, None]), 0.0, neg).astype(F32)

    def page_spec(j):
        return pl.BlockSpec(
            (1, 1, page_size, n_heads, hw),
            lambda bi, c, pt: (layer, pt[bi * n_pages + c * n_pages_step + j], 0, 0, 0))

    per_b = lambda shape: pl.BlockSpec((1,) + shape[1:], lambda bi, c, pt: (bi,) + (0,) * (len(shape) - 1))
    const = lambda a: pl.BlockSpec(a.shape, lambda bi, c, pt: (0,) * a.ndim)
    small = [const(a) for a in lams] + [const(gsub_row)]
    grid_spec = pltpu.PrefetchScalarGridSpec(
        num_scalar_prefetch=1,
        grid=(nb, n_pages // n_pages_step),
        in_specs=[per_b(qz.shape), per_b(kn.shape), per_b(vn.shape), const(bias), const(biasn)] + small
                 + [page_spec(j) for j in range(n_pages_step)] * 2,
        out_specs=pl.BlockSpec((1, n_rows // 2, hw), lambda bi, c, pt: (bi, 0, 0)),
        scratch_shapes=[pltpu.VMEM((n_rows, 1), F32), pltpu.VMEM((n_rows, 1), F32),
                        pltpu.VMEM((n_rows, hw), F32)],
    )
    return pl.pallas_call(
        functools.partial(_decode_attn_kernel, n_pages_step=n_pages_step, lam_init=lam_init),
        grid_spec=grid_spec,
        out_shape=jax.ShapeDtypeStruct((nb, n_rows // 2, hw), F32),
        compiler_params=pltpu.CompilerParams(dimension_semantics=("arbitrary", "arbitrary"),
                                             vmem_limit_bytes=V7X_VMEM_LIMIT),
        name="decode_attn",
    )(pt_flat, qz, kn, vn, bias, biasn, *lams, gsub_row,
      *([cache_k] * n_pages_step), *([cache_v] * n_pages_step))


def _out_mlp(an, cin, sga, sgc, x, wa, wc, wo, gm, wup, wdn, tm):
    n, d = x.shape
    row_spec = pl.BlockSpec((tm, d), lambda i: (i, 0))
    weights = (wa, wc, wo, gm, wup, wdn)
    return pl.pallas_call(
        _out_mlp_kernel,
        grid=(n // tm,),
        in_specs=[row_spec] * 5 + [_resident(w.shape) for w in weights],
        out_specs=row_spec,
        out_shape=jax.ShapeDtypeStruct((n, d), F32),
        compiler_params=pltpu.CompilerParams(dimension_semantics=("arbitrary",),
                                             vmem_limit_bytes=V7X_VMEM_LIMIT),
        name="out_mlp",
    )(an, cin, sga, sgc, x, *weights)


def _layer(l, yp, ys, cache_k, cache_v, state_conv, page_table, p):
    bp, tp, d = yp.shape
    bs, ts, _ = ys.shape
    past = page_table.shape[1] * cache_k.shape[2]
    lam_init = 0.8 - 0.6 * math.exp(-0.3 * l)
    row = lambda a: a.reshape(1, -1).astype(F32)

    gn, gm = row(p["g_attn_norm"][l]), row(p["g_mlp_norm"][l])
    gq = row(jnp.tile(p["g_q"][l], 2 * N_HEADS))
    gk = row(jnp.tile(p["g_k"][l], 2 * N_HEADS))
    gsub_row = row(p["g_sub"][l])
    gsub_col = p["g_sub"][l].reshape(-1, 1).astype(F32)
    lams = [row(p[n][l]) for n in ("lambda_q1", "lambda_k1", "lambda_q2", "lambda_k2")]
    w_in, wa, wc, wo, wup, wdn = (p[n][l].astype(BF16) for n in
                                  ("w_in", "w_attn_out", "w_conv_out", "w_o", "w_up", "w_down"))
    conv_w = p["conv_w"][l].astype(F32)
    gmat = _group_sum_matrix()

    cos_p, sin_p = _rope_tables(jnp.arange(tp))
    tm = _row_tile(tp, 256)
    kf, vf, qb, kb, vb, cin, sga, sgc, cst_p = _prompt_inproj(
        yp, gn, w_in, gq, gk, gmat, cos_p, sin_p, conv_w, tm)
    an = _prompt_attn(qb, kb, vb, lams, gsub_col, lam_init, _row_tile(tp, 256))
    flat = lambda a: a.reshape(bp * tp, d)
    yp_new = _out_mlp(flat(an), flat(cin), flat(sga), flat(sgc), flat(yp), wa, wc, wo, gm, wup, wdn,
                      _row_tile(bp * tp, 256)).reshape(bp, tp, d)

    cos_s, sin_s = _rope_tables(past + jnp.arange(ts))
    cos_s, sin_s = jnp.repeat(cos_s, bs, axis=0), jnp.repeat(sin_s, bs, axis=0)
    x_tb = jnp.swapaxes(ys, 0, 1).reshape(ts * bs, d)
    st_tb = jnp.swapaxes(state_conv[l].astype(F32), 0, 1).reshape((CONV_W - 1) * bs, d)
    q_s, k_s, v_s, cin_s, sga_s, sgc_s, cst_s = _sample_inproj(
        x_tb, gn, w_in, gq, gk, gmat, cos_s, sin_s, conv_w, st_tb, bs)
    to_bt = lambda a: jnp.swapaxes(a.reshape(ts, bs, N_HEADS, HEAD_W), 0, 1)
    k_new, v_new = to_bt(k_s), to_bt(v_s)
    q5 = jnp.transpose(q_s.reshape(ts, bs, N_HEADS, 2, D_QK), (1, 3, 2, 0, 4))
    q5 = q5.reshape(bs, 2, N_HEADS * ts, D_QK)
    zeros = jnp.zeros_like(q5[:, 0])
    qz = jnp.stack([jnp.concatenate([q5[:, 0], zeros], axis=-1),
                    jnp.concatenate([zeros, q5[:, 1]], axis=-1)], axis=1)
    qz = qz.reshape(bs, 2 * N_HEADS * ts, HEAD_W)
    pad = ((0, 0), (0, NEW_KEY_PAD - ts), (0, 0), (0, 0))
    n_pages = page_table.shape[1]
    n_pages_step = 8 if n_pages % 8 == 0 else 1
    o_s = _decode_attn(page_table, qz, jnp.pad(k_new, pad), jnp.pad(v_new, pad), cache_k, cache_v, l,
                       lams, gsub_row, lam_init, n_pages_step)
    an_s = jnp.transpose(o_s.reshape(bs, N_HEADS, ts, D_V), (2, 0, 1, 3)).reshape(ts * bs, d)
    ys_tb = _out_mlp(an_s.astype(BF16), cin_s, sga_s, sgc_s, x_tb, wa, wc, wo, gm, wup, wdn,
                     _row_tile(ts * bs, 512))
    ys_new = jnp.swapaxes(ys_tb.reshape(ts, bs, d), 0, 1)
    cst_s = jnp.swapaxes(cst_s.reshape(CONV_W - 1, bs, d), 0, 1)

    heads = lambda a: a.reshape(bp, tp, N_HEADS, HEAD_W)
    return yp_new, ys_new, (heads(kf), heads(vf), cst_p, k_new, v_new, cst_s)


def kernel(x_prompt, x_sample, cache_k, cache_v, state_conv, page_table, g_attn_norm, w_in, g_q, g_k,
           lambda_q1, lambda_k1, lambda_q2, lambda_k2, g_sub, w_attn_out, conv_w, w_conv_out, w_o,
           g_mlp_norm, w_up, w_down):
    params = dict(g_attn_norm=g_attn_norm, w_in=w_in, g_q=g_q, g_k=g_k, lambda_q1=lambda_q1,
                  lambda_k1=lambda_k1, lambda_q2=lambda_q2, lambda_k2=lambda_k2, g_sub=g_sub,
                  w_attn_out=w_attn_out, conv_w=conv_w, w_conv_out=w_conv_out, w_o=w_o,
                  g_mlp_norm=g_mlp_norm, w_up=w_up, w_down=w_down)
    yp, ys = x_prompt, x_sample
    caches = []
    for l in range(w_in.shape[0]):
        yp, ys, new = _layer(l, yp, ys, cache_k, cache_v, state_conv, page_table, params)
        caches.append(new)
    stacked = [jnp.stack([c[i] for c in caches]) for i in range(6)]
    return (yp, ys, *stacked)
```

```python
import functools
import math

import jax
import jax.numpy as jnp
from jax import lax
from jax.experimental import pallas as pl
from jax.experimental.pallas import tpu as pltpu

F32 = jnp.float32
BF16 = jnp.bfloat16

N_HEADS = 8
D_QK = 64
D_V = 2 * D_QK
HEAD_W = 2 * D_QK
CONV_W = 3
ROPE_THETA = 10000.0
EPS = 1e-6
SCALE = 1.0 / math.sqrt(D_QK)
LOG2_E = math.log2(math.e)
N_IN_CHUNKS = 8
NEW_KEY_PAD = 16

V7X_VMEM_LIMIT = 56 * 1024 * 1024
INPROJ_ROWS = 512
OUT_MLP_ROWS = 512
ATTN_ROWS = 256
DECODE_PAGES_PER_STEP = 16
DECODE_PAGE_BUFFERS = 2
GROUP_TILE = 256

NT_DIMS = (((1,), (1,)), ((), ()))
TN_DIMS = (((0,), (0,)), ((), ()))


def _dot(a, b):
    return jnp.dot(a, b, preferred_element_type=F32)


def _rms_rows(x, g_row):
    return x * lax.rsqrt(jnp.mean(x * x, axis=-1, keepdims=True) + EPS) * g_row


def _lambda(lq1_ref, lk1_ref, lq2_ref, lk2_ref, lam_init):
    s1 = jnp.sum(lq1_ref[...] * lk1_ref[...], axis=-1, keepdims=True)
    s2 = jnp.sum(lq2_ref[...] * lk2_ref[...], axis=-1, keepdims=True)
    return jnp.exp(s1) - jnp.exp(s2) + lam_init


def _qk_norm_rope(z, g_row, gmat, cos, sin_signed, upper_half):
    zz = (z * z).astype(BF16)
    n_tiles = z.shape[1] // GROUP_TILE
    ss = jnp.concatenate(
        [_dot(zz[:, c * GROUP_TILE:(c + 1) * GROUP_TILE], gmat) for c in range(n_tiles)], axis=1)
    y = z * lax.rsqrt(ss * (1.0 / D_QK) + EPS) * g_row
    outs = []
    for h in range(N_HEADS):
        ys = y[:, h * HEAD_W:(h + 1) * HEAD_W]
        partner = jnp.where(upper_half, pltpu.roll(ys, D_QK // 2, 1),
                            pltpu.roll(ys, HEAD_W - D_QK // 2, 1))
        outs.append(ys * cos + partner * sin_signed)
    return jnp.concatenate(outs, axis=1)


def _inproj_common(x, gn_ref, w_ref, gq_ref, gk_ref, gmat_ref, cos_ref, sin_ref):
    rows, d = x.shape
    rs = lax.rsqrt(jnp.mean(x * x, axis=-1, keepdims=True) + EPS)
    xg = (x * gn_ref[...]).astype(BF16)
    zc = lambda c: _dot(xg, w_ref[:, c * d:(c + 1) * d]) * rs
    cos, sin_signed = cos_ref[...], sin_ref[...]
    upper_half = (lax.broadcasted_iota(jnp.int32, (rows, HEAD_W), 1) & (D_QK // 2)) != 0
    gmat = gmat_ref[...]
    q = _qk_norm_rope(zc(0), gq_ref[...], gmat, cos, sin_signed, upper_half)
    k = _qk_norm_rope(zc(1), gk_ref[...], gmat, cos, sin_signed, upper_half)
    u = zc(3) * zc(5)
    u_b = zc(4)
    sga = jax.nn.sigmoid(zc(6))
    sgc = jax.nn.sigmoid(zc(7))
    v = zc(2)
    return q, k, v, u, u_b, sga, sgc


def _prompt_inproj_kernel(x_ref, gn_ref, w_ref, gq_ref, gk_ref, gmat_ref, cos_ref, sin_ref, cw_ref,
                          kf_ref, vf_ref, qb_ref, kb_ref, vb_ref, cin_ref, sga_ref, sgc_ref, cst_ref,
                          carry_ref):
    @pl.when(pl.program_id(1) == 0)
    def _():
        carry_ref[...] = jnp.zeros_like(carry_ref)

    x = x_ref[0]
    tm = x.shape[0]
    q, k, v, u, u_b, sga, sgc = _inproj_common(x, gn_ref, w_ref, gq_ref, gk_ref, gmat_ref,
                                               cos_ref, sin_ref)
    qb_ref[0] = (q * (SCALE * LOG2_E)).astype(BF16)
    kf_ref[0] = k
    kb_ref[0] = k.astype(BF16)

    prev = carry_ref[...]
    c0, c1 = prev[6:7], prev[7:8]
    rows = lax.broadcasted_iota(jnp.int32, (tm, 1), 0)
    u1 = jnp.where(rows == 0, c1, pltpu.roll(u, 1, 0))
    u2 = jnp.where(rows == 0, c0, jnp.where(rows == 1, c1, pltpu.roll(u, 2, 0)))
    cw = cw_ref[...]
    cy = cw[0:1] * u2 + cw[1:2] * u1 + cw[2:3] * u
    carry_ref[...] = u[tm - 8:tm]
    cst_ref[0] = u[tm - (CONV_W - 1):tm]
    cin_ref[0] = (u_b * cy).astype(BF16)
    sga_ref[0] = sga.astype(BF16)
    sgc_ref[0] = sgc.astype(BF16)
    vf_ref[0] = v
    vb_ref[0] = v.astype(BF16)


def _sample_inproj_kernel(x_ref, gn_ref, w_ref, gq_ref, gk_ref, gmat_ref, cos_ref, sin_ref, cw_ref,
                          st_ref, qf_ref, kf_ref, vf_ref, cin_ref, sga_ref, sgc_ref, cst_ref, *, nb):
    x = x_ref[...]
    n = x.shape[0]
    q, k, v, u, u_b, sga, sgc = _inproj_common(x, gn_ref, w_ref, gq_ref, gk_ref, gmat_ref,
                                               cos_ref, sin_ref)
    qf_ref[...] = q
    kf_ref[...] = k
    vf_ref[...] = v
    ue = jnp.concatenate([st_ref[...], u], axis=0)
    cw = cw_ref[...]
    cy = cw[0:1] * ue[0:n] + cw[1:2] * ue[nb:nb + n] + cw[2:3] * ue[2 * nb:2 * nb + n]
    cst_ref[...] = ue[n:n + (CONV_W - 1) * nb]
    cin_ref[...] = (u_b * cy).astype(BF16)
    sga_ref[...] = sga.astype(BF16)
    sgc_ref[...] = sgc.astype(BF16)


def _prompt_attn_kernel(q_ref, k_ref, v_ref, lq1_ref, lk1_ref, lq2_ref, lk2_ref, gsub_ref,
                        o_ref, qs_ref, acc_ref, m_ref, l_ref, *, tq, lam_init):
    qi = pl.program_id(1)
    lam = _lambda(lq1_ref, lk1_ref, lq2_ref, lk2_ref, lam_init)
    lane = lax.broadcasted_iota(jnp.int32, (tq, HEAD_W), 1)
    key_row = lax.broadcasted_iota(jnp.int32, (tq, 2 * tq), 0)
    q_col = lax.broadcasted_iota(jnp.int32, (tq, 2 * tq), 1)
    causal = key_row <= jnp.where(q_col >= tq, q_col - tq, q_col)
    head_lanes = [slice(h * HEAD_W, (h + 1) * HEAD_W) for h in range(N_HEADS)]

    for h, hs in enumerate(head_lanes):
        qh = q_ref[0, :, hs]
        zero = jnp.zeros_like(qh)
        qs_ref[h] = jnp.concatenate(
            [jnp.where(lane < D_QK, qh, zero), jnp.where(lane >= D_QK, qh, zero)], axis=0)
    m_ref[...] = jnp.full_like(m_ref, -jnp.inf)
    l_ref[...] = jnp.zeros_like(l_ref)
    acc_ref[...] = jnp.zeros_like(acc_ref)

    def step(kt, masked):
        ks = pl.ds(pl.multiple_of(kt * tq, tq), tq)
        scores = [lax.dot_general(k_ref[0, ks, hs], qs_ref[h], NT_DIMS, preferred_element_type=F32)
                  for h, hs in enumerate(head_lanes)]
        for h, hs in enumerate(head_lanes):
            s = scores[h]
            if masked:
                s = jnp.where(causal, s, -jnp.inf)
            m_prev = m_ref[h]
            m_new = jnp.maximum(m_prev, jnp.max(s, axis=0, keepdims=True))
            corr = jnp.exp2(m_prev - m_new)
            p = jnp.exp2(s - m_new)
            l_ref[h] = l_ref[h] * corr + jnp.sum(p, axis=0, keepdims=True)
            pv = lax.dot_general(v_ref[0, ks, hs], p.astype(BF16), TN_DIMS,
                                 preferred_element_type=F32)
            acc_ref[h] = acc_ref[h] * corr + pv
            m_ref[h] = m_new

    def body(kt, carry):
        step(kt, False)
        return carry

    lax.fori_loop(0, qi, body, 0)
    step(qi, True)

    for h, hs in enumerate(head_lanes):
        o = acc_ref[h] * (1.0 / l_ref[h])
        od = o[:, :tq] - lam * o[:, tq:]
        ms = jnp.mean(od * od, axis=0, keepdims=True)
        on = od * lax.rsqrt(ms + EPS) * gsub_ref[...] * (1.0 - lam_init)
        o_ref[0, :, hs] = on.T.astype(BF16)


def _decode_attn_kernel(pt_ref, qz_ref, kn_ref, vn_ref, bias_ref, biasn_ref,
                        lq1_ref, lk1_ref, lq2_ref, lk2_ref, gsub_ref, *rest, n_pages_step, lam_init):
    k_refs = rest[:n_pages_step]
    v_refs = rest[n_pages_step:2 * n_pages_step]
    o_ref, m_ref, l_ref, acc_ref = rest[2 * n_pages_step:]
    c = pl.program_id(1)
    n_rows = qz_ref.shape[1]

    @pl.when(c == 0)
    def _():
        m_ref[...] = jnp.full_like(m_ref, -jnp.inf)
        l_ref[...] = jnp.zeros_like(l_ref)
        acc_ref[...] = jnp.zeros_like(acc_ref)

    qz = (qz_ref[0] * SCALE).astype(BF16)

    def scores(k3, bias):
        kp = k3.reshape(k3.shape[0] * k3.shape[1], k3.shape[2]).astype(BF16)
        return lax.dot_general(qz, kp, NT_DIMS, preferred_element_type=F32) + bias

    def update(s_list, v_list):
        m_prev = m_ref[...]
        m_new = m_prev
        for s in s_list:
            m_new = jnp.maximum(m_new, jnp.max(s, axis=1, keepdims=True))
        corr = jnp.exp(m_prev - m_new)
        l_new = l_ref[...] * corr
        acc = acc_ref[...] * corr
        for s, v3 in zip(s_list, v_list):
            p = jnp.exp(s - m_new)
            l_new = l_new + jnp.sum(p, axis=1, keepdims=True)
            vp = v3.reshape(v3.shape[0] * v3.shape[1], v3.shape[2]).astype(BF16)
            acc = acc + _dot(p.astype(BF16), vp)
        m_ref[...] = m_new
        l_ref[...] = l_new
        acc_ref[...] = acc

    bias = bias_ref[...]
    update([scores(k_refs[j][0, 0], bias) for j in range(n_pages_step)],
           [v_refs[j][0, 0] for j in range(n_pages_step)])

    @pl.when(c == pl.num_programs(1) - 1)
    def _():
        update([scores(kn_ref[0], biasn_ref[...])], [vn_ref[0]])
        lam = _lambda(lq1_ref, lk1_ref, lq2_ref, lk2_ref, lam_init)
        o = acc_ref[...] * (1.0 / l_ref[...])
        od = o[:n_rows // 2] - lam * o[n_rows // 2:]
        o_ref[0] = _rms_rows(od, gsub_ref[...]) * (1.0 - lam_init)


def _out_mlp_kernel(an_ref, cin_ref, sga_ref, sgc_ref, x_ref, wa_ref, wc_ref, wo_ref, gm_ref,
                    wup_ref, wdn_ref, y_ref):
    a = _dot(an_ref[...], wa_ref[...])
    cv = _dot(cin_ref[...], wc_ref[...])
    h = sga_ref[...].astype(F32) * a + sgc_ref[...].astype(F32) * cv
    y = x_ref[...] + _dot(h.astype(BF16), wo_ref[...])
    xn = _rms_rows(y, gm_ref[...]).astype(BF16)
    d = y.shape[1]
    out = y
    for c in range(wup_ref.shape[1] // d):
        hc = _dot(xn, wup_ref[:, c * d:(c + 1) * d])
        hc = jnp.square(jnp.maximum(hc, 0.0)).astype(BF16)
        out = out + _dot(hc, wdn_ref[c * d:(c + 1) * d, :])
    y_ref[...] = out


def _resident(shape):
    nd = len(shape)
    return pl.BlockSpec(shape, lambda *_: (0,) * nd, pipeline_mode=pl.Buffered(1))


def _rope_tables(pos):
    inv = ROPE_THETA ** (-jnp.arange(0, D_QK, 2, dtype=F32) / D_QK)
    ang = pos.astype(F32)[:, None] * inv[None, :]
    cos, sin = jnp.cos(ang), jnp.sin(ang)
    return jnp.tile(cos, (1, 4)), jnp.tile(jnp.concatenate([-sin, sin], axis=1), (1, 2))


def _group_sum_matrix():
    r = jnp.arange(GROUP_TILE) // D_QK
    return (r[:, None] == r[None, :]).astype(BF16)


def _row_tile(n, target):
    t = min(n, target)
    while n % t:
        t //= 2
    return t


def _prompt_inproj(x, gn, w_in, gq, gk, gmat, cos, sin_signed, conv_w, tm):
    b, t, d = x.shape
    tile = lambda dt: jax.ShapeDtypeStruct((b, t, d), dt)
    row_spec = pl.BlockSpec((1, tm, d), lambda bi, i: (bi, i, 0))
    tab_spec = pl.BlockSpec((tm, HEAD_W), lambda bi, i: (i, 0))
    return pl.pallas_call(
        _prompt_inproj_kernel,
        grid=(b, t // tm),
        in_specs=[row_spec, _resident(gn.shape), _resident(w_in.shape), _resident(gq.shape),
                  _resident(gk.shape), _resident(gmat.shape), tab_spec, tab_spec,
                  _resident(conv_w.shape)],
        out_specs=[row_spec] * 8 + [pl.BlockSpec((1, CONV_W - 1, d), lambda bi, i: (bi, 0, 0))],
        out_shape=[tile(F32), tile(F32), tile(BF16), tile(BF16), tile(BF16), tile(BF16), tile(BF16),
                   tile(BF16), jax.ShapeDtypeStruct((b, CONV_W - 1, d), F32)],
        scratch_shapes=[pltpu.VMEM((8, d), F32)],
        compiler_params=pltpu.CompilerParams(dimension_semantics=("arbitrary", "arbitrary"),
                                             vmem_limit_bytes=V7X_VMEM_LIMIT),
        name="prompt_inproj",
    )(x, gn, w_in, gq, gk, gmat, cos, sin_signed, conv_w)


def _sample_inproj(x, gn, w_in, gq, gk, gmat, cos, sin_signed, conv_w, state, nb):
    n, d = x.shape
    full = lambda dt: jax.ShapeDtypeStruct((n, d), dt)
    args = (x, gn, w_in, gq, gk, gmat, cos, sin_signed, conv_w, state)
    return pl.pallas_call(
        functools.partial(_sample_inproj_kernel, nb=nb),
        out_shape=[full(F32), full(F32), full(F32), full(BF16), full(BF16), full(BF16),
                   jax.ShapeDtypeStruct(((CONV_W - 1) * nb, d), F32)],
        compiler_params=pltpu.CompilerParams(vmem_limit_bytes=V7X_VMEM_LIMIT),
        name="sample_inproj",
    )(*args)


def _prompt_attn(qb, kb, vb, lams, gsub_col, lam_init, tq):
    b, t, d = qb.shape
    q_spec = pl.BlockSpec((1, tq, d), lambda bi, qi: (bi, qi, 0))
    kv_spec = pl.BlockSpec((1, t, d), lambda bi, qi: (bi, 0, 0))
    small = [_resident(a.shape) for a in lams] + [_resident(gsub_col.shape)]
    return pl.pallas_call(
        functools.partial(_prompt_attn_kernel, tq=tq, lam_init=lam_init),
        grid=(b, t // tq),
        in_specs=[q_spec, kv_spec, kv_spec] + small,
        out_specs=q_spec,
        out_shape=jax.ShapeDtypeStruct((b, t, d), BF16),
        scratch_shapes=[pltpu.VMEM((N_HEADS, 2 * tq, HEAD_W), BF16),
                        pltpu.VMEM((N_HEADS, D_V, 2 * tq), F32),
                        pltpu.VMEM((N_HEADS, 1, 2 * tq), F32), pltpu.VMEM((N_HEADS, 1, 2 * tq), F32)],
        compiler_params=pltpu.CompilerParams(dimension_semantics=("arbitrary", "arbitrary"),
                                             vmem_limit_bytes=V7X_VMEM_LIMIT),
        name="prompt_attn",
    )(qb, kb, vb, *lams, gsub_col)


def _decode_attn(page_table, qz, kn, vn, cache_k, cache_v, layer, lams, gsub_row, lam_init,
                 n_pages_step):
    nb, n_rows, _ = qz.shape
    n_pages = page_table.shape[1]
    page_size, n_heads, hw = cache_k.shape[2:]
    pt_flat = page_table.reshape(-1)
    row_head = (jnp.arange(n_rows) % (n_rows // 2)) // (n_rows // 2 // n_heads)
    row_tok = jnp.arange(n_rows) % (n_rows // 2 // n_heads)
    col_head = jnp.arange(page_size * n_heads) % n_heads
    neg = jnp.asarray(-jnp.inf, F32)
    bias = jnp.where(row_head[:, None] == col_head[None, :], 0.0, neg).astype(F32)
    coln = jnp.arange(NEW_KEY_PAD * n_heads)
    biasn = jnp.where((row_head[:, None] == (coln % n_heads)[None, :])
                      & ((coln // n_heads)[None, :] <= row_tok[:, None]), 0.0, neg).astype(F32)

    def page_spec(j):
        return pl.BlockSpec(
            (1, 1, page_size, n_heads, hw),
            lambda bi, c, pt: (layer, pt[bi * n_pages + c * n_pages_step + j], 0, 0, 0),
            pipeline_mode=pl.Buffered(DECODE_PAGE_BUFFERS))

    per_b = lambda shape: pl.BlockSpec((1,) + shape[1:], lambda bi, c, pt: (bi,) + (0,) * (len(shape) - 1))
    const = lambda a: pl.BlockSpec(a.shape, lambda bi, c, pt: (0,) * a.ndim)
    small = [const(a) for a in lams] + [const(gsub_row)]
    grid_spec = pltpu.PrefetchScalarGridSpec(
        num_scalar_prefetch=1,
        grid=(nb, n_pages // n_pages_step),
        in_specs=[per_b(qz.shape), per_b(kn.shape), per_b(vn.shape), const(bias), const(biasn)] + small
                 + [page_spec(j) for j in range(n_pages_step)] * 2,
        out_specs=pl.BlockSpec((1, n_rows // 2, hw), lambda bi, c, pt: (bi, 0, 0)),
        scratch_shapes=[pltpu.VMEM((n_rows, 1), F32), pltpu.VMEM((n_rows, 1), F32),
                        pltpu.VMEM((n_rows, hw), F32)],
    )
    return pl.pallas_call(
        functools.partial(_decode_attn_kernel, n_pages_step=n_pages_step, lam_init=lam_init),
        grid_spec=grid_spec,
        out_shape=jax.ShapeDtypeStruct((nb, n_rows // 2, hw), F32),
        compiler_params=pltpu.CompilerParams(dimension_semantics=("arbitrary", "arbitrary"),
                                             vmem_limit_bytes=V7X_VMEM_LIMIT),
        name="decode_attn",
    )(pt_flat, qz, kn, vn, bias, biasn, *lams, gsub_row,
      *([cache_k] * n_pages_step), *([cache_v] * n_pages_step))


def _out_mlp(an, cin, sga, sgc, x, wa, wc, wo, gm, wup, wdn, tm):
    n, d = x.shape
    row_spec = pl.BlockSpec((tm, d), lambda i: (i, 0))
    weights = (wa, wc, wo, gm, wup, wdn)
    return pl.pallas_call(
        _out_mlp_kernel,
        grid=(n // tm,),
        in_specs=[row_spec] * 5 + [_resident(w.shape) for w in weights],
        out_specs=row_spec,
        out_shape=jax.ShapeDtypeStruct((n, d), F32),
        compiler_params=pltpu.CompilerParams(dimension_semantics=("arbitrary",),
                                             vmem_limit_bytes=V7X_VMEM_LIMIT),
        name="out_mlp",
    )(an, cin, sga, sgc, x, *weights)


def _layer(l, yp, ys, cache_k, cache_v, state_conv, page_table, p):
    bp, tp, d = yp.shape
    bs, ts, _ = ys.shape
    past = page_table.shape[1] * cache_k.shape[2]
    lam_init = 0.8 - 0.6 * math.exp(-0.3 * l)
    row = lambda a: a.reshape(1, -1).astype(F32)

    gn, gm = row(p["g_attn_norm"][l]), row(p["g_mlp_norm"][l])
    gq = row(jnp.tile(p["g_q"][l], 2 * N_HEADS))
    gk = row(jnp.tile(p["g_k"][l], 2 * N_HEADS))
    gsub_row = row(p["g_sub"][l])
    gsub_col = p["g_sub"][l].reshape(-1, 1).astype(F32)
    lams = [row(p[n][l]) for n in ("lambda_q1", "lambda_k1", "lambda_q2", "lambda_k2")]
    w_in, wa, wc, wo, wup, wdn = (p[n][l].astype(BF16) for n in
                                  ("w_in", "w_attn_out", "w_conv_out", "w_o", "w_up", "w_down"))
    conv_w = p["conv_w"][l].astype(F32)
    gmat = _group_sum_matrix()

    cos_p, sin_p = _rope_tables(jnp.arange(tp))
    tm = _row_tile(tp, INPROJ_ROWS)
    kf, vf, qb, kb, vb, cin, sga, sgc, cst_p = _prompt_inproj(
        yp, gn, w_in, gq, gk, gmat, cos_p, sin_p, conv_w, tm)
    an = _prompt_attn(qb, kb, vb, lams, gsub_col, lam_init, _row_tile(tp, ATTN_ROWS))
    flat = lambda a: a.reshape(bp * tp, d)
    yp_new = _out_mlp(flat(an), flat(cin), flat(sga), flat(sgc), flat(yp), wa, wc, wo, gm, wup, wdn,
                      _row_tile(bp * tp, OUT_MLP_ROWS)).reshape(bp, tp, d)

    cos_s, sin_s = _rope_tables(past + jnp.arange(ts))
    cos_s, sin_s = jnp.repeat(cos_s, bs, axis=0), jnp.repeat(sin_s, bs, axis=0)
    x_tb = jnp.swapaxes(ys, 0, 1).reshape(ts * bs, d)
    st_tb = jnp.swapaxes(state_conv[l].astype(F32), 0, 1).reshape((CONV_W - 1) * bs, d)
    q_s, k_s, v_s, cin_s, sga_s, sgc_s, cst_s = _sample_inproj(
        x_tb, gn, w_in, gq, gk, gmat, cos_s, sin_s, conv_w, st_tb, bs)
    to_bt = lambda a: jnp.swapaxes(a.reshape(ts, bs, N_HEADS, HEAD_W), 0, 1)
    k_new, v_new = to_bt(k_s), to_bt(v_s)
    q5 = jnp.transpose(q_s.reshape(ts, bs, N_HEADS, 2, D_QK), (1, 3, 2, 0, 4))
    q5 = q5.reshape(bs, 2, N_HEADS * ts, D_QK)
    zeros = jnp.zeros_like(q5[:, 0])
    qz = jnp.stack([jnp.concatenate([q5[:, 0], zeros], axis=-1),
                    jnp.concatenate([zeros, q5[:, 1]], axis=-1)], axis=1)
    qz = qz.reshape(bs, 2 * N_HEADS * ts, HEAD_W)
    pad = ((0, 0), (0, NEW_KEY_PAD - ts), (0, 0), (0, 0))
    n_pages = page_table.shape[1]
    n_pages_step = math.gcd(n_pages, DECODE_PAGES_PER_STEP)
    o_s = _decode_attn(page_table, qz, jnp.pad(k_new, pad), jnp.pad(v_new, pad), cache_k, cache_v, l,
                       lams, gsub_row, lam_init, n_pages_step)
    an_s = jnp.transpose(o_s.reshape(bs, N_HEADS, ts, D_V), (2, 0, 1, 3)).reshape(ts * bs, d)
    ys_tb = _out_mlp(an_s.astype(BF16), cin_s, sga_s, sgc_s, x_tb, wa, wc, wo, gm, wup, wdn,
                     _row_tile(ts * bs, 512))
    ys_new = jnp.swapaxes(ys_tb.reshape(ts, bs, d), 0, 1)
    cst_s = jnp.swapaxes(cst_s.reshape(CONV_W - 1, bs, d), 0, 1)

    heads = lambda a: a.reshape(bp, tp, N_HEADS, HEAD_W)
    return yp_new, ys_new, (heads(kf), heads(vf), cst_p, k_new, v_new, cst_s)


def kernel(x_prompt, x_sample, cache_k, cache_v, state_conv, page_table, g_attn_norm, w_in, g_q, g_k,
           lambda_q1, lambda_k1, lambda_q2, lambda_k2, g_sub, w_attn_out, conv_w, w_conv_out, w_o,
           g_mlp_norm, w_up, w_down):
    params = dict(g_attn_norm=g_attn_norm, w_in=w_in, g_q=g_q, g_k=g_k, lambda_q1=lambda_q1,
                  lambda_k1=lambda_k1, lambda_q2=lambda_q2, lambda_k2=lambda_k2, g_sub=g_sub,
                  w_attn_out=w_attn_out, conv_w=conv_w, w_conv_out=w_conv_out, w_o=w_o,
                  g_mlp_norm=g_mlp_norm, w_up=w_up, w_down=w_down)
    yp, ys = x_prompt, x_sample
    caches = []
    for l in range(w_in.shape[0]):
        yp, ys, new = _layer(l, yp, ys, cache_k, cache_v, state_conv, page_table, params)
        caches.append(new)
    stacked = [jnp.stack([c[i] for c in caches]) for i in range(6)]
    return (yp, ys, *stacked)
```

```python
import functools
import math

import jax
import jax.numpy as jnp
from jax import lax
from jax.experimental import pallas as pl
from jax.experimental.pallas import tpu as pltpu

F32 = jnp.float32
BF16 = jnp.bfloat16

N_HEADS = 8
D_QK = 64
D_V = 2 * D_QK
HEAD_W = 2 * D_QK
CONV_W = 3
ROPE_THETA = 10000.0
EPS = 1e-6
SCALE = 1.0 / math.sqrt(D_QK)
LOG2_E = math.log2(math.e)
N_IN_CHUNKS = 8
NEW_KEY_PAD = 16

V7X_VMEM_LIMIT = 56 * 1024 * 1024
INPROJ_ROWS = 512
OUT_MLP_ROWS = 512
ATTN_ROWS = 256
DECODE_PAGES_PER_STEP = 16
DECODE_SLOTS = 2
DECODE_GROUP_PAGES = 8
GROUP_TILE = 256

NT_DIMS = (((1,), (1,)), ((), ()))
TN_DIMS = (((0,), (0,)), ((), ()))


def _dot(a, b):
    return jnp.dot(a, b, preferred_element_type=F32)


def _rms_rows(x, g_row):
    return x * lax.rsqrt(jnp.mean(x * x, axis=-1, keepdims=True) + EPS) * g_row


def _lambda(lq1_ref, lk1_ref, lq2_ref, lk2_ref, lam_init):
    s1 = jnp.sum(lq1_ref[...] * lk1_ref[...], axis=-1, keepdims=True)
    s2 = jnp.sum(lq2_ref[...] * lk2_ref[...], axis=-1, keepdims=True)
    return jnp.exp(s1) - jnp.exp(s2) + lam_init


def _qk_norm_rope(z, g_row, gmat, cos, sin_signed, upper_half):
    zz = (z * z).astype(BF16)
    n_tiles = z.shape[1] // GROUP_TILE
    ss = jnp.concatenate(
        [_dot(zz[:, c * GROUP_TILE:(c + 1) * GROUP_TILE], gmat) for c in range(n_tiles)], axis=1)
    y = z * lax.rsqrt(ss * (1.0 / D_QK) + EPS) * g_row
    outs = []
    for h in range(N_HEADS):
        ys = y[:, h * HEAD_W:(h + 1) * HEAD_W]
        partner = jnp.where(upper_half, pltpu.roll(ys, D_QK // 2, 1),
                            pltpu.roll(ys, HEAD_W - D_QK // 2, 1))
        outs.append(ys * cos + partner * sin_signed)
    return jnp.concatenate(outs, axis=1)


def _inproj_common(x, gn_ref, w_ref, gq_ref, gk_ref, gmat_ref, cos_ref, sin_ref):
    rows, d = x.shape
    rs = lax.rsqrt(jnp.mean(x * x, axis=-1, keepdims=True) + EPS)
    xg = (x * gn_ref[...]).astype(BF16)
    zc = lambda c: _dot(xg, w_ref[:, c * d:(c + 1) * d]) * rs
    cos, sin_signed = cos_ref[...], sin_ref[...]
    upper_half = (lax.broadcasted_iota(jnp.int32, (rows, HEAD_W), 1) & (D_QK // 2)) != 0
    gmat = gmat_ref[...]
    q = _qk_norm_rope(zc(0), gq_ref[...], gmat, cos, sin_signed, upper_half)
    k = _qk_norm_rope(zc(1), gk_ref[...], gmat, cos, sin_signed, upper_half)
    u = zc(3) * zc(5)
    u_b = zc(4)
    sga = jax.nn.sigmoid(zc(6))
    sgc = jax.nn.sigmoid(zc(7))
    v = zc(2)
    return q, k, v, u, u_b, sga, sgc


def _prompt_inproj_kernel(x_ref, gn_ref, w_ref, gq_ref, gk_ref, gmat_ref, cos_ref, sin_ref, cw_ref,
                          kf_ref, vf_ref, qb_ref, kb_ref, vb_ref, cin_ref, sga_ref, sgc_ref, cst_ref,
                          carry_ref):
    @pl.when(pl.program_id(1) == 0)
    def _():
        carry_ref[...] = jnp.zeros_like(carry_ref)

    x = x_ref[0]
    tm = x.shape[0]
    q, k, v, u, u_b, sga, sgc = _inproj_common(x, gn_ref, w_ref, gq_ref, gk_ref, gmat_ref,
                                               cos_ref, sin_ref)
    qb_ref[0] = (q * (SCALE * LOG2_E)).astype(BF16)
    kf_ref[0] = k
    kb_ref[0] = k.astype(BF16)

    prev = carry_ref[...]
    c0, c1 = prev[6:7], prev[7:8]
    rows = lax.broadcasted_iota(jnp.int32, (tm, 1), 0)
    u1 = jnp.where(rows == 0, c1, pltpu.roll(u, 1, 0))
    u2 = jnp.where(rows == 0, c0, jnp.where(rows == 1, c1, pltpu.roll(u, 2, 0)))
    cw = cw_ref[...]
    cy = cw[0:1] * u2 + cw[1:2] * u1 + cw[2:3] * u
    carry_ref[...] = u[tm - 8:tm]
    cst_ref[0] = u[tm - (CONV_W - 1):tm]
    cin_ref[0] = (u_b * cy).astype(BF16)
    sga_ref[0] = sga.astype(BF16)
    sgc_ref[0] = sgc.astype(BF16)
    vf_ref[0] = v
    vb_ref[0] = v.astype(BF16)


def _sample_inproj_kernel(x_ref, gn_ref, w_ref, gq_ref, gk_ref, gmat_ref, cos_ref, sin_ref, cw_ref,
                          st_ref, qf_ref, kf_ref, vf_ref, cin_ref, sga_ref, sgc_ref, cst_ref, *, nb):
    x = x_ref[...]
    n = x.shape[0]
    q, k, v, u, u_b, sga, sgc = _inproj_common(x, gn_ref, w_ref, gq_ref, gk_ref, gmat_ref,
                                               cos_ref, sin_ref)
    qf_ref[...] = q
    kf_ref[...] = k
    vf_ref[...] = v
    ue = jnp.concatenate([st_ref[...], u], axis=0)
    cw = cw_ref[...]
    cy = cw[0:1] * ue[0:n] + cw[1:2] * ue[nb:nb + n] + cw[2:3] * ue[2 * nb:2 * nb + n]
    cst_ref[...] = ue[n:n + (CONV_W - 1) * nb]
    cin_ref[...] = (u_b * cy).astype(BF16)
    sga_ref[...] = sga.astype(BF16)
    sgc_ref[...] = sgc.astype(BF16)


def _prompt_attn_kernel(q_ref, k_ref, v_ref, lq1_ref, lk1_ref, lq2_ref, lk2_ref, gsub_ref,
                        o_ref, qs_ref, acc_ref, m_ref, l_ref, *, tq, lam_init):
    qi = pl.program_id(1)
    lam = _lambda(lq1_ref, lk1_ref, lq2_ref, lk2_ref, lam_init)
    lane = lax.broadcasted_iota(jnp.int32, (tq, HEAD_W), 1)
    key_row = lax.broadcasted_iota(jnp.int32, (tq, 2 * tq), 0)
    q_col = lax.broadcasted_iota(jnp.int32, (tq, 2 * tq), 1)
    causal = key_row <= jnp.where(q_col >= tq, q_col - tq, q_col)
    head_lanes = [slice(h * HEAD_W, (h + 1) * HEAD_W) for h in range(N_HEADS)]

    for h, hs in enumerate(head_lanes):
        qh = q_ref[0, :, hs]
        zero = jnp.zeros_like(qh)
        qs_ref[h] = jnp.concatenate(
            [jnp.where(lane < D_QK, qh, zero), jnp.where(lane >= D_QK, qh, zero)], axis=0)
    m_ref[...] = jnp.full_like(m_ref, -jnp.inf)
    l_ref[...] = jnp.zeros_like(l_ref)
    acc_ref[...] = jnp.zeros_like(acc_ref)

    def step(kt, masked):
        ks = pl.ds(pl.multiple_of(kt * tq, tq), tq)
        scores = [lax.dot_general(k_ref[0, ks, hs], qs_ref[h], NT_DIMS, preferred_element_type=F32)
                  for h, hs in enumerate(head_lanes)]
        for h, hs in enumerate(head_lanes):
            s = scores[h]
            if masked:
                s = jnp.where(causal, s, -jnp.inf)
            m_prev = m_ref[h]
            m_new = jnp.maximum(m_prev, jnp.max(s, axis=0, keepdims=True))
            corr = jnp.exp2(m_prev - m_new)
            p = jnp.exp2(s - m_new)
            l_ref[h] = l_ref[h] * corr + jnp.sum(p, axis=0, keepdims=True)
            pv = lax.dot_general(v_ref[0, ks, hs], p.astype(BF16), TN_DIMS,
                                 preferred_element_type=F32)
            acc_ref[h] = acc_ref[h] * corr + pv
            m_ref[h] = m_new

    def body(kt, carry):
        step(kt, False)
        return carry

    lax.fori_loop(0, qi, body, 0)
    step(qi, True)

    for h, hs in enumerate(head_lanes):
        o = acc_ref[h] * (1.0 / l_ref[h])
        od = o[:, :tq] - lam * o[:, tq:]
        ms = jnp.mean(od * od, axis=0, keepdims=True)
        on = od * lax.rsqrt(ms + EPS) * gsub_ref[...] * (1.0 - lam_init)
        o_ref[0, :, hs] = on.T.astype(BF16)


def _decode_attn_kernel(pt_ref, qz_ref, kn_ref, vn_ref, bias_ref, biasn_ref,
                        lq1_ref, lk1_ref, lq2_ref, lk2_ref, gsub_ref, k_hbm, v_hbm,
                        o_ref, kbuf, vbuf, sem, m_ref, l_ref, acc_ref, *, layer, lam_init):
    n_slots, n_pages_step = kbuf.shape[:2]
    c = pl.program_id(1)
    nc = pl.num_programs(1)
    step = pl.program_id(0) * nc + c
    total = pl.num_programs(0) * nc
    n_rows = qz_ref.shape[1]

    def page_copies(s, slot):
        copies = []
        for j in range(n_pages_step):
            page = pt_ref[s * n_pages_step + j]
            copies.append(pltpu.make_async_copy(k_hbm.at[layer, page], kbuf.at[slot, j], sem.at[0, slot]))
            copies.append(pltpu.make_async_copy(v_hbm.at[layer, page], vbuf.at[slot, j], sem.at[1, slot]))
        return copies

    @pl.when(step == 0)
    def _():
        for s in range(n_slots - 1):
            @pl.when(s < total)
            def _():
                for cp in page_copies(s, s):
                    cp.start()

    ahead = step + (n_slots - 1)

    @pl.when(ahead < total)
    def _():
        for cp in page_copies(ahead, ahead % n_slots):
            cp.start()

    @pl.when(c == 0)
    def _():
        m_ref[...] = jnp.full_like(m_ref, -jnp.inf)
        l_ref[...] = jnp.zeros_like(l_ref)
        acc_ref[...] = jnp.zeros_like(acc_ref)

    slot = step % n_slots
    for cp in page_copies(step, slot):
        cp.wait()

    qz = (qz_ref[0] * (SCALE * LOG2_E)).astype(BF16)

    def scores(k3, bias):
        kp = k3.reshape(k3.shape[0] * k3.shape[1], k3.shape[2]).astype(BF16)
        return lax.dot_general(qz, kp, NT_DIMS, preferred_element_type=F32) + bias

    def update(s_list, v_list):
        m_prev = m_ref[...]
        m_new = m_prev
        for s in s_list:
            m_new = jnp.maximum(m_new, jnp.max(s, axis=1, keepdims=True))
        corr = jnp.exp2(m_prev - m_new)
        l_new = l_ref[...] * corr
        acc = acc_ref[...] * corr
        for s, v3 in zip(s_list, v_list):
            p = jnp.exp2(s - m_new)
            l_new = l_new + jnp.sum(p, axis=1, keepdims=True)
            vp = v3.reshape(v3.shape[0] * v3.shape[1], v3.shape[2]).astype(BF16)
            acc = acc + _dot(p.astype(BF16), vp)
        m_ref[...] = m_new
        l_ref[...] = l_new
        acc_ref[...] = acc

    bias = bias_ref[...]
    group = math.gcd(n_pages_step, DECODE_GROUP_PAGES)
    s_all = [scores(kbuf[slot, j], bias) for j in range(n_pages_step)]
    for g in range(0, n_pages_step, group):
        update(s_all[g:g + group], [vbuf[slot, j] for j in range(g, g + group)])

    @pl.when(c == nc - 1)
    def _():
        update([scores(kn_ref[0], biasn_ref[...])], [vn_ref[0]])
        lam = _lambda(lq1_ref, lk1_ref, lq2_ref, lk2_ref, lam_init)
        o = acc_ref[...] * (1.0 / l_ref[...])
        od = o[:n_rows // 2] - lam * o[n_rows // 2:]
        o_ref[0] = _rms_rows(od, gsub_ref[...]) * (1.0 - lam_init)


def _out_mlp_kernel(an_ref, cin_ref, sga_ref, sgc_ref, x_ref, wa_ref, wc_ref, wo_ref, gm_ref,
                    wup_ref, wdn_ref, y_ref):
    a = _dot(an_ref[...], wa_ref[...])
    cv = _dot(cin_ref[...], wc_ref[...])
    h = sga_ref[...].astype(F32) * a + sgc_ref[...].astype(F32) * cv
    y = x_ref[...] + _dot(h.astype(BF16), wo_ref[...])
    xn = _rms_rows(y, gm_ref[...]).astype(BF16)
    d = y.shape[1]
    out = y
    for c in range(wup_ref.shape[1] // d):
        hc = _dot(xn, wup_ref[:, c * d:(c + 1) * d])
        hc = jnp.square(jnp.maximum(hc, 0.0)).astype(BF16)
        out = out + _dot(hc, wdn_ref[c * d:(c + 1) * d, :])
    y_ref[...] = out


def _resident(shape):
    nd = len(shape)
    return pl.BlockSpec(shape, lambda *_: (0,) * nd, pipeline_mode=pl.Buffered(1))


def _rope_tables(pos):
    inv = ROPE_THETA ** (-jnp.arange(0, D_QK, 2, dtype=F32) / D_QK)
    ang = pos.astype(F32)[:, None] * inv[None, :]
    cos, sin = jnp.cos(ang), jnp.sin(ang)
    return jnp.tile(cos, (1, 4)), jnp.tile(jnp.concatenate([-sin, sin], axis=1), (1, 2))


def _group_sum_matrix():
    r = jnp.arange(GROUP_TILE) // D_QK
    return (r[:, None] == r[None, :]).astype(BF16)


def _row_tile(n, target):
    t = min(n, target)
    while n % t:
        t //= 2
    return t


def _prompt_inproj(x, gn, w_in, gq, gk, gmat, cos, sin_signed, conv_w, tm):
    b, t, d = x.shape
    tile = lambda dt: jax.ShapeDtypeStruct((b, t, d), dt)
    row_spec = pl.BlockSpec((1, tm, d), lambda bi, i: (bi, i, 0))
    tab_spec = pl.BlockSpec((tm, HEAD_W), lambda bi, i: (i, 0))
    return pl.pallas_call(
        _prompt_inproj_kernel,
        grid=(b, t // tm),
        in_specs=[row_spec, _resident(gn.shape), _resident(w_in.shape), _resident(gq.shape),
                  _resident(gk.shape), _resident(gmat.shape), tab_spec, tab_spec,
                  _resident(conv_w.shape)],
        out_specs=[row_spec] * 8 + [pl.BlockSpec((1, CONV_W - 1, d), lambda bi, i: (bi, 0, 0))],
        out_shape=[tile(F32), tile(F32), tile(BF16), tile(BF16), tile(BF16), tile(BF16), tile(BF16),
                   tile(BF16), jax.ShapeDtypeStruct((b, CONV_W - 1, d), F32)],
        scratch_shapes=[pltpu.VMEM((8, d), F32)],
        compiler_params=pltpu.CompilerParams(dimension_semantics=("arbitrary", "arbitrary"),
                                             vmem_limit_bytes=V7X_VMEM_LIMIT),
        name="prompt_inproj",
    )(x, gn, w_in, gq, gk, gmat, cos, sin_signed, conv_w)


def _sample_inproj(x, gn, w_in, gq, gk, gmat, cos, sin_signed, conv_w, state, nb):
    n, d = x.shape
    full = lambda dt: jax.ShapeDtypeStruct((n, d), dt)
    args = (x, gn, w_in, gq, gk, gmat, cos, sin_signed, conv_w, state)
    return pl.pallas_call(
        functools.partial(_sample_inproj_kernel, nb=nb),
        out_shape=[full(F32), full(F32), full(F32), full(BF16), full(BF16), full(BF16),
                   jax.ShapeDtypeStruct(((CONV_W - 1) * nb, d), F32)],
        compiler_params=pltpu.CompilerParams(vmem_limit_bytes=V7X_VMEM_LIMIT),
        name="sample_inproj",
    )(*args)


def _prompt_attn(qb, kb, vb, lams, gsub_col, lam_init, tq):
    b, t, d = qb.shape
    q_spec = pl.BlockSpec((1, tq, d), lambda bi, qi: (bi, qi, 0))
    kv_spec = pl.BlockSpec((1, t, d), lambda bi, qi: (bi, 0, 0))
    small = [_resident(a.shape) for a in lams] + [_resident(gsub_col.shape)]
    return pl.pallas_call(
        functools.partial(_prompt_attn_kernel, tq=tq, lam_init=lam_init),
        grid=(b, t // tq),
        in_specs=[q_spec, kv_spec, kv_spec] + small,
        out_specs=q_spec,
        out_shape=jax.ShapeDtypeStruct((b, t, d), BF16),
        scratch_shapes=[pltpu.VMEM((N_HEADS, 2 * tq, HEAD_W), BF16),
                        pltpu.VMEM((N_HEADS, D_V, 2 * tq), F32),
                        pltpu.VMEM((N_HEADS, 1, 2 * tq), F32), pltpu.VMEM((N_HEADS, 1, 2 * tq), F32)],
        compiler_params=pltpu.CompilerParams(dimension_semantics=("arbitrary", "arbitrary"),
                                             vmem_limit_bytes=V7X_VMEM_LIMIT),
        name="prompt_attn",
    )(qb, kb, vb, *lams, gsub_col)


def _decode_attn(page_table, qz, kn, vn, cache_k, cache_v, layer, lams, gsub_row, lam_init,
                 n_pages_step):
    nb, n_rows, _ = qz.shape
    n_pages = page_table.shape[1]
    page_size, n_heads, hw = cache_k.shape[2:]
    pt_flat = page_table.reshape(-1)
    row_head = (jnp.arange(n_rows) % (n_rows // 2)) // (n_rows // 2 // n_heads)
    row_tok = jnp.arange(n_rows) % (n_rows // 2 // n_heads)
    col_head = jnp.arange(page_size * n_heads) % n_heads
    neg = jnp.asarray(-jnp.inf, F32)
    bias = jnp.where(row_head[:, None] == col_head[None, :], 0.0, neg).astype(F32)
    coln = jnp.arange(NEW_KEY_PAD * n_heads)
    biasn = jnp.where((row_head[:, None] == (coln % n_heads)[None, :])
                      & ((coln // n_heads)[None, :] <= row_tok[:, None]), 0.0, neg).astype(F32)

    per_b = lambda shape: pl.BlockSpec((1,) + shape[1:], lambda bi, c, pt: (bi,) + (0,) * (len(shape) - 1))
    const = lambda a: pl.BlockSpec(a.shape, lambda bi, c, pt: (0,) * a.ndim)
    small = [const(a) for a in lams] + [const(gsub_row)]
    grid_spec = pltpu.PrefetchScalarGridSpec(
        num_scalar_prefetch=1,
        grid=(nb, n_pages // n_pages_step),
        in_specs=[per_b(qz.shape), per_b(kn.shape), per_b(vn.shape), const(bias), const(biasn)] + small
                 + [pl.BlockSpec(memory_space=pl.ANY)] * 2,
        out_specs=pl.BlockSpec((1, n_rows // 2, hw), lambda bi, c, pt: (bi, 0, 0)),
        scratch_shapes=[pltpu.VMEM((DECODE_SLOTS, n_pages_step, page_size, n_heads, hw), cache_k.dtype),
                        pltpu.VMEM((DECODE_SLOTS, n_pages_step, page_size, n_heads, hw), cache_v.dtype),
                        pltpu.SemaphoreType.DMA((2, DECODE_SLOTS)),
                        pltpu.VMEM((n_rows, 1), F32), pltpu.VMEM((n_rows, 1), F32),
                        pltpu.VMEM((n_rows, hw), F32)],
    )
    return pl.pallas_call(
        functools.partial(_decode_attn_kernel, layer=layer, lam_init=lam_init),
        grid_spec=grid_spec,
        out_shape=jax.ShapeDtypeStruct((nb, n_rows // 2, hw), F32),
        compiler_params=pltpu.CompilerParams(dimension_semantics=("arbitrary", "arbitrary"),
                                             vmem_limit_bytes=V7X_VMEM_LIMIT),
        name="decode_attn",
    )(pt_flat, qz, kn, vn, bias, biasn, *lams, gsub_row, cache_k, cache_v)


def _out_mlp(an, cin, sga, sgc, x, wa, wc, wo, gm, wup, wdn, tm):
    n, d = x.shape
    row_spec = pl.BlockSpec((tm, d), lambda i: (i, 0))
    weights = (wa, wc, wo, gm, wup, wdn)
    return pl.pallas_call(
        _out_mlp_kernel,
        grid=(n // tm,),
        in_specs=[row_spec] * 5 + [_resident(w.shape) for w in weights],
        out_specs=row_spec,
        out_shape=jax.ShapeDtypeStruct((n, d), F32),
        compiler_params=pltpu.CompilerParams(dimension_semantics=("arbitrary",),
                                             vmem_limit_bytes=V7X_VMEM_LIMIT),
        name="out_mlp",
    )(an, cin, sga, sgc, x, *weights)


def _layer(l, yp, ys, cache_k, cache_v, state_conv, page_table, p):
    bp, tp, d = yp.shape
    bs, ts, _ = ys.shape
    past = page_table.shape[1] * cache_k.shape[2]
    lam_init = 0.8 - 0.6 * math.exp(-0.3 * l)
    row = lambda a: a.reshape(1, -1).astype(F32)

    gn, gm = row(p["g_attn_norm"][l]), row(p["g_mlp_norm"][l])
    gq = row(jnp.tile(p["g_q"][l], 2 * N_HEADS))
    gk = row(jnp.tile(p["g_k"][l], 2 * N_HEADS))
    gsub_row = row(p["g_sub"][l])
    gsub_col = p["g_sub"][l].reshape(-1, 1).astype(F32)
    lams = [row(p[n][l]) for n in ("lambda_q1", "lambda_k1", "lambda_q2", "lambda_k2")]
    w_in, wa, wc, wo, wup, wdn = (p[n][l].astype(BF16) for n in
                                  ("w_in", "w_attn_out", "w_conv_out", "w_o", "w_up", "w_down"))
    conv_w = p["conv_w"][l].astype(F32)
    gmat = _group_sum_matrix()

    cos_p, sin_p = _rope_tables(jnp.arange(tp))
    tm = _row_tile(tp, INPROJ_ROWS)
    kf, vf, qb, kb, vb, cin, sga, sgc, cst_p = _prompt_inproj(
        yp, gn, w_in, gq, gk, gmat, cos_p, sin_p, conv_w, tm)
    an = _prompt_attn(qb, kb, vb, lams, gsub_col, lam_init, _row_tile(tp, ATTN_ROWS))
    flat = lambda a: a.reshape(bp * tp, d)
    yp_new = _out_mlp(flat(an), flat(cin), flat(sga), flat(sgc), flat(yp), wa, wc, wo, gm, wup, wdn,
                      _row_tile(bp * tp, OUT_MLP_ROWS)).reshape(bp, tp, d)

    cos_s, sin_s = _rope_tables(past + jnp.arange(ts))
    cos_s, sin_s = jnp.repeat(cos_s, bs, axis=0), jnp.repeat(sin_s, bs, axis=0)
    x_tb = jnp.swapaxes(ys, 0, 1).reshape(ts * bs, d)
    st_tb = jnp.swapaxes(state_conv[l].astype(F32), 0, 1).reshape((CONV_W - 1) * bs, d)
    q_s, k_s, v_s, cin_s, sga_s, sgc_s, cst_s = _sample_inproj(
        x_tb, gn, w_in, gq, gk, gmat, cos_s, sin_s, conv_w, st_tb, bs)
    to_bt = lambda a: jnp.swapaxes(a.reshape(ts, bs, N_HEADS, HEAD_W), 0, 1)
    k_new, v_new = to_bt(k_s), to_bt(v_s)
    q5 = jnp.transpose(q_s.reshape(ts, bs, N_HEADS, 2, D_QK), (1, 3, 2, 0, 4))
    q5 = q5.reshape(bs, 2, N_HEADS * ts, D_QK)
    zeros = jnp.zeros_like(q5[:, 0])
    qz = jnp.stack([jnp.concatenate([q5[:, 0], zeros], axis=-1),
                    jnp.concatenate([zeros, q5[:, 1]], axis=-1)], axis=1)
    qz = qz.reshape(bs, 2 * N_HEADS * ts, HEAD_W)
    pad = ((0, 0), (0, NEW_KEY_PAD - ts), (0, 0), (0, 0))
    n_pages = page_table.shape[1]
    n_pages_step = math.gcd(n_pages, DECODE_PAGES_PER_STEP)
    o_s = _decode_attn(page_table, qz, jnp.pad(k_new, pad), jnp.pad(v_new, pad), cache_k, cache_v, l,
                       lams, gsub_row, lam_init, n_pages_step)
    an_s = jnp.transpose(o_s.reshape(bs, N_HEADS, ts, D_V), (2, 0, 1, 3)).reshape(ts * bs, d)
    ys_tb = _out_mlp(an_s.astype(BF16), cin_s, sga_s, sgc_s, x_tb, wa, wc, wo, gm, wup, wdn,
                     _row_tile(ts * bs, 512))
    ys_new = jnp.swapaxes(ys_tb.reshape(ts, bs, d), 0, 1)
    cst_s = jnp.swapaxes(cst_s.reshape(CONV_W - 1, bs, d), 0, 1)

    heads = lambda a: a.reshape(bp, tp, N_HEADS, HEAD_W)
    return yp_new, ys_new, (heads(kf), heads(vf), cst_p, k_new, v_new, cst_s)


def kernel(x_prompt, x_sample, cache_k, cache_v, state_conv, page_table, g_attn_norm, w_in, g_q, g_k,
           lambda_q1, lambda_k1, lambda_q2, lambda_k2, g_sub, w_attn_out, conv_w, w_conv_out, w_o,
           g_mlp_norm, w_up, w_down):
    params = dict(g_attn_norm=g_attn_norm, w_in=w_in, g_q=g_q, g_k=g_k, lambda_q1=lambda_q1,
                  lambda_k1=lambda_k1, lambda_q2=lambda_q2, lambda_k2=lambda_k2, g_sub=g_sub,
                  w_attn_out=w_attn_out, conv_w=conv_w, w_conv_out=w_conv_out, w_o=w_o,
                  g_mlp_norm=g_mlp_norm, w_up=w_up, w_down=w_down)
    yp, ys = x_prompt, x_sample
    caches = []
    for l in range(w_in.shape[0]):
        yp, ys, new = _layer(l, yp, ys, cache_k, cache_v, state_conv, page_table, params)
        caches.append(new)
    stacked = [jnp.stack([c[i] for c in caches]) for i in range(6)]
    return (yp, ys, *stacked)
```

```python
import functools
import math

import jax
import jax.numpy as jnp
from jax import lax
from jax.experimental import pallas as pl
from jax.experimental.pallas import tpu as pltpu

F32 = jnp.float32
BF16 = jnp.bfloat16

N_HEADS = 8
D_QK = 64
D_V = 2 * D_QK
HEAD_W = 2 * D_QK
CONV_W = 3
ROPE_THETA = 10000.0
EPS = 1e-6
SCALE = 1.0 / math.sqrt(D_QK)
LOG2_E = math.log2(math.e)
N_IN_CHUNKS = 8
NEW_KEY_PAD = 16

V7X_VMEM_LIMIT = 56 * 1024 * 1024
INPROJ_ROWS = 512
OUT_MLP_ROWS = 256
ATTN_ROWS = 256
DECODE_PAGES_PER_STEP = 8
DECODE_SLOTS = 2
DECODE_GROUP_PAGES = 8
N_MLP_INPUTS = 11
N_DECODE_INPUTS = 12
GROUP_TILE = 256

NT_DIMS = (((1,), (1,)), ((), ()))
TN_DIMS = (((0,), (0,)), ((), ()))


def _dot(a, b):
    return jnp.dot(a, b, preferred_element_type=F32)


def _rms_rows(x, g_row):
    return x * lax.rsqrt(jnp.mean(x * x, axis=-1, keepdims=True) + EPS) * g_row


def _lambda(lq1_ref, lk1_ref, lq2_ref, lk2_ref, lam_init):
    s1 = jnp.sum(lq1_ref[...] * lk1_ref[...], axis=-1, keepdims=True)
    s2 = jnp.sum(lq2_ref[...] * lk2_ref[...], axis=-1, keepdims=True)
    return jnp.exp(s1) - jnp.exp(s2) + lam_init


def _qk_norm_rope(z, g_row, gmat, cos, sin_signed, upper_half):
    zz = (z * z).astype(BF16)
    n_tiles = z.shape[1] // GROUP_TILE
    ss = jnp.concatenate(
        [_dot(zz[:, c * GROUP_TILE:(c + 1) * GROUP_TILE], gmat) for c in range(n_tiles)], axis=1)
    y = z * lax.rsqrt(ss * (1.0 / D_QK) + EPS) * g_row
    outs = []
    for h in range(N_HEADS):
        ys = y[:, h * HEAD_W:(h + 1) * HEAD_W]
        partner = jnp.where(upper_half, pltpu.roll(ys, D_QK // 2, 1),
                            pltpu.roll(ys, HEAD_W - D_QK // 2, 1))
        outs.append(ys * cos + partner * sin_signed)
    return jnp.concatenate(outs, axis=1)


def _inproj_common(x, gn_ref, w_ref, gq_ref, gk_ref, gmat_ref, cos_ref, sin_ref):
    rows, d = x.shape
    rs = lax.rsqrt(jnp.mean(x * x, axis=-1, keepdims=True) + EPS)
    xg = (x * gn_ref[...]).astype(BF16)
    zc = lambda c: _dot(xg, w_ref[:, c * d:(c + 1) * d]) * rs
    cos, sin_signed = cos_ref[...], sin_ref[...]
    upper_half = (lax.broadcasted_iota(jnp.int32, (rows, HEAD_W), 1) & (D_QK // 2)) != 0
    gmat = gmat_ref[...]
    q = _qk_norm_rope(zc(0), gq_ref[...], gmat, cos, sin_signed, upper_half)
    k = _qk_norm_rope(zc(1), gk_ref[...], gmat, cos, sin_signed, upper_half)
    u = zc(3) * zc(5)
    u_b = zc(4)
    sga = jax.nn.sigmoid(zc(6))
    sgc = jax.nn.sigmoid(zc(7))
    v = zc(2)
    return q, k, v, u, u_b, sga, sgc


def _prompt_inproj_kernel(x_ref, gn_ref, w_ref, gq_ref, gk_ref, gmat_ref, cos_ref, sin_ref, cw_ref,
                          kf_ref, vf_ref, qb_ref, kb_ref, vb_ref, cin_ref, sga_ref, sgc_ref, cst_ref,
                          carry_ref):
    @pl.when(pl.program_id(1) == 0)
    def _():
        carry_ref[...] = jnp.zeros_like(carry_ref)

    x = x_ref[0]
    tm = x.shape[0]
    q, k, v, u, u_b, sga, sgc = _inproj_common(x, gn_ref, w_ref, gq_ref, gk_ref, gmat_ref,
                                               cos_ref, sin_ref)
    qb_ref[0] = (q * (SCALE * LOG2_E)).astype(BF16)
    kf_ref[0] = k
    kb_ref[0] = k.astype(BF16)

    prev = carry_ref[...]
    c0, c1 = prev[6:7], prev[7:8]
    rows = lax.broadcasted_iota(jnp.int32, (tm, 1), 0)
    u1 = jnp.where(rows == 0, c1, pltpu.roll(u, 1, 0))
    u2 = jnp.where(rows == 0, c0, jnp.where(rows == 1, c1, pltpu.roll(u, 2, 0)))
    cw = cw_ref[...]
    cy = cw[0:1] * u2 + cw[1:2] * u1 + cw[2:3] * u
    carry_ref[...] = u[tm - 8:tm]
    cst_ref[0] = u[tm - (CONV_W - 1):tm]
    cin_ref[0] = (u_b * cy).astype(BF16)
    sga_ref[0] = sga.astype(BF16)
    sgc_ref[0] = sgc.astype(BF16)
    vf_ref[0] = v
    vb_ref[0] = v.astype(BF16)


def _sample_inproj_kernel(x_ref, gn_ref, w_ref, gq_ref, gk_ref, gmat_ref, cos_ref, sin_ref, cw_ref,
                          st_ref, qf_ref, kf_ref, vf_ref, cin_ref, sga_ref, sgc_ref, cst_ref, *, nb):
    x = x_ref[...]
    n = x.shape[0]
    q, k, v, u, u_b, sga, sgc = _inproj_common(x, gn_ref, w_ref, gq_ref, gk_ref, gmat_ref,
                                               cos_ref, sin_ref)
    qf_ref[...] = q
    kf_ref[...] = k
    vf_ref[...] = v
    ue = jnp.concatenate([st_ref[...], u], axis=0)
    cw = cw_ref[...]
    cy = cw[0:1] * ue[0:n] + cw[1:2] * ue[nb:nb + n] + cw[2:3] * ue[2 * nb:2 * nb + n]
    cst_ref[...] = ue[n:n + (CONV_W - 1) * nb]
    cin_ref[...] = (u_b * cy).astype(BF16)
    sga_ref[...] = sga.astype(BF16)
    sgc_ref[...] = sgc.astype(BF16)


def _prompt_attn_kernel(q_ref, k_ref, v_ref, lq1_ref, lk1_ref, lq2_ref, lk2_ref, gsub_ref,
                        o_ref, qs_ref, acc_ref, m_ref, l_ref, *, tq, lam_init):
    qi = pl.program_id(1)
    lam = _lambda(lq1_ref, lk1_ref, lq2_ref, lk2_ref, lam_init)
    lane = lax.broadcasted_iota(jnp.int32, (tq, HEAD_W), 1)
    key_row = lax.broadcasted_iota(jnp.int32, (tq, 2 * tq), 0)
    q_col = lax.broadcasted_iota(jnp.int32, (tq, 2 * tq), 1)
    causal = key_row <= jnp.where(q_col >= tq, q_col - tq, q_col)
    head_lanes = [slice(h * HEAD_W, (h + 1) * HEAD_W) for h in range(N_HEADS)]

    for h, hs in enumerate(head_lanes):
        qh = q_ref[0, :, hs]
        zero = jnp.zeros_like(qh)
        qs_ref[h] = jnp.concatenate(
            [jnp.where(lane < D_QK, qh, zero), jnp.where(lane >= D_QK, qh, zero)], axis=0)
    m_ref[...] = jnp.full_like(m_ref, -jnp.inf)
    l_ref[...] = jnp.zeros_like(l_ref)
    acc_ref[...] = jnp.zeros_like(acc_ref)

    def step(kt, masked):
        ks = pl.ds(pl.multiple_of(kt * tq, tq), tq)
        scores = [lax.dot_general(k_ref[0, ks, hs], qs_ref[h], NT_DIMS, preferred_element_type=F32)
                  for h, hs in enumerate(head_lanes)]
        for h, hs in enumerate(head_lanes):
            s = scores[h]
            if masked:
                s = jnp.where(causal, s, -jnp.inf)
            m_prev = m_ref[h]
            m_new = jnp.maximum(m_prev, jnp.max(s, axis=0, keepdims=True))
            corr = jnp.exp2(m_prev - m_new)
            p = jnp.exp2(s - m_new)
            l_ref[h] = l_ref[h] * corr + jnp.sum(p, axis=0, keepdims=True)
            pv = lax.dot_general(v_ref[0, ks, hs], p.astype(BF16), TN_DIMS,
                                 preferred_element_type=F32)
            acc_ref[h] = acc_ref[h] * corr + pv
            m_ref[h] = m_new

    def body(kt, carry):
        step(kt, False)
        return carry

    lax.fori_loop(0, qi, body, 0)
    step(qi, True)

    for h, hs in enumerate(head_lanes):
        o = acc_ref[h] * (1.0 / l_ref[h])
        od = o[:, :tq] - lam * o[:, tq:]
        ms = jnp.mean(od * od, axis=0, keepdims=True)
        on = od * lax.rsqrt(ms + EPS) * gsub_ref[...] * (1.0 - lam_init)
        o_ref[0, :, hs] = on.T.astype(BF16)


def _decode_step(step, total, nc, pt_ref, qz_ref, kn_ref, vn_ref, bias_ref, biasn_ref,
                 lq1_ref, lk1_ref, lq2_ref, lk2_ref, gsub_ref, k_hbm, v_hbm,
                 o_ref, kbuf, vbuf, sem, m_ref, l_ref, acc_ref, *, layer, lam_init):
    n_slots, n_pages_step = kbuf.shape[:2]
    c = step % nc
    n_rows = qz_ref.shape[1]

    def page_copies(s, slot):
        copies = []
        for j in range(n_pages_step):
            page = pt_ref[s * n_pages_step + j]
            copies.append(pltpu.make_async_copy(k_hbm.at[layer, page], kbuf.at[slot, j], sem.at[0, slot]))
            copies.append(pltpu.make_async_copy(v_hbm.at[layer, page], vbuf.at[slot, j], sem.at[1, slot]))
        return copies

    @pl.when(step == 0)
    def _():
        for s in range(n_slots - 1):
            @pl.when(s < total)
            def _():
                for cp in page_copies(s, s):
                    cp.start()

    ahead = step + (n_slots - 1)

    @pl.when(ahead < total)
    def _():
        for cp in page_copies(ahead, ahead % n_slots):
            cp.start()

    @pl.when(c == 0)
    def _():
        m_ref[...] = jnp.full_like(m_ref, -jnp.inf)
        l_ref[...] = jnp.zeros_like(l_ref)
        acc_ref[...] = jnp.zeros_like(acc_ref)

    slot = step % n_slots
    for cp in page_copies(step, slot):
        cp.wait()

    qz = (qz_ref[0] * (SCALE * LOG2_E)).astype(BF16)

    def scores(k3, bias):
        kp = k3.reshape(k3.shape[0] * k3.shape[1], k3.shape[2]).astype(BF16)
        return lax.dot_general(qz, kp, NT_DIMS, preferred_element_type=F32) + bias

    def update(s_list, v_list):
        m_prev = m_ref[...]
        m_new = m_prev
        for s in s_list:
            m_new = jnp.maximum(m_new, jnp.max(s, axis=1, keepdims=True))
        corr = jnp.exp2(m_prev - m_new)
        l_new = l_ref[...] * corr
        acc = acc_ref[...] * corr
        for s, v3 in zip(s_list, v_list):
            p = jnp.exp2(s - m_new)
            l_new = l_new + jnp.sum(p, axis=1, keepdims=True)
            vp = v3.reshape(v3.shape[0] * v3.shape[1], v3.shape[2]).astype(BF16)
            acc = acc + _dot(p.astype(BF16), vp)
        m_ref[...] = m_new
        l_ref[...] = l_new
        acc_ref[...] = acc

    bias = bias_ref[...]
    group = math.gcd(n_pages_step, DECODE_GROUP_PAGES)
    s_all = [scores(kbuf[slot, j], bias) for j in range(n_pages_step)]
    for g in range(0, n_pages_step, group):
        update(s_all[g:g + group], [vbuf[slot, j] for j in range(g, g + group)])

    @pl.when(c == nc - 1)
    def _():
        update([scores(kn_ref[0], biasn_ref[...])], [vn_ref[0]])
        lam = _lambda(lq1_ref, lk1_ref, lq2_ref, lk2_ref, lam_init)
        o = acc_ref[...] * (1.0 / l_ref[...])
        od = o[:n_rows // 2] - lam * o[n_rows // 2:]
        o_ref[0] = _rms_rows(od, gsub_ref[...]) * (1.0 - lam_init)


def _out_mlp_phases(an_ref, cin_ref, sga_ref, sgc_ref, x_ref, wa_ref, wc_ref, wo_ref, gm_ref,
                    wup_ref, wdn_ref, y_ref, xn_ref):
    d = x_ref.shape[1]

    def mix():
        a = _dot(an_ref[...], wa_ref[...])
        cv = _dot(cin_ref[...], wc_ref[...])
        h = sga_ref[...].astype(F32) * a + sgc_ref[...].astype(F32) * cv
        y = x_ref[...] + _dot(h.astype(BF16), wo_ref[...])
        xn_ref[...] = _rms_rows(y, gm_ref[...]).astype(BF16)
        y_ref[...] = y

    def mlp_chunk(c):
        def run():
            hc = _dot(xn_ref[...], wup_ref[:, c * d:(c + 1) * d])
            hc = jnp.square(jnp.maximum(hc, 0.0)).astype(BF16)
            y_ref[...] += _dot(hc, wdn_ref[c * d:(c + 1) * d, :])
        return run

    return [mix] + [mlp_chunk(c) for c in range(wup_ref.shape[1] // d)]


def _out_mlp_kernel(*refs):
    for phase in _out_mlp_phases(*refs):
        phase()


def _out_mlp_decode_kernel(pt_ref, *refs, n_sub, nc, layer, lam_init):
    n_in = N_MLP_INPUTS + N_DECODE_INPUTS
    mlp_in, dec_in = refs[:N_MLP_INPUTS], refs[N_MLP_INPUTS:n_in]
    y_ref, o_ref, xn_ref = refs[n_in:n_in + 3]
    dec_refs = dec_in + (o_ref,) + refs[n_in + 3:]
    phases = _out_mlp_phases(*mlp_in, y_ref, xn_ref)
    total = pl.num_programs(0) * n_sub
    for k in range(n_sub):
        _decode_step(pl.program_id(0) * n_sub + k, total, nc, pt_ref, *dec_refs,
                     layer=layer, lam_init=lam_init)
        for p, phase in enumerate(phases):
            if p * n_sub // len(phases) == k:
                phase()


def _resident(shape):
    nd = len(shape)
    return pl.BlockSpec(shape, lambda *_: (0,) * nd, pipeline_mode=pl.Buffered(1))


def _rope_tables(pos):
    inv = ROPE_THETA ** (-jnp.arange(0, D_QK, 2, dtype=F32) / D_QK)
    ang = pos.astype(F32)[:, None] * inv[None, :]
    cos, sin = jnp.cos(ang), jnp.sin(ang)
    return jnp.tile(cos, (1, 4)), jnp.tile(jnp.concatenate([-sin, sin], axis=1), (1, 2))


def _group_sum_matrix():
    r = jnp.arange(GROUP_TILE) // D_QK
    return (r[:, None] == r[None, :]).astype(BF16)


def _row_tile(n, target):
    t = min(n, target)
    while n % t:
        t //= 2
    return t


def _prompt_inproj(x, gn, w_in, gq, gk, gmat, cos, sin_signed, conv_w, tm):
    b, t, d = x.shape
    tile = lambda dt: jax.ShapeDtypeStruct((b, t, d), dt)
    row_spec = pl.BlockSpec((1, tm, d), lambda bi, i: (bi, i, 0))
    tab_spec = pl.BlockSpec((tm, HEAD_W), lambda bi, i: (i, 0))
    return pl.pallas_call(
        _prompt_inproj_kernel,
        grid=(b, t // tm),
        in_specs=[row_spec, _resident(gn.shape), _resident(w_in.shape), _resident(gq.shape),
                  _resident(gk.shape), _resident(gmat.shape), tab_spec, tab_spec,
                  _resident(conv_w.shape)],
        out_specs=[row_spec] * 8 + [pl.BlockSpec((1, CONV_W - 1, d), lambda bi, i: (bi, 0, 0))],
        out_shape=[tile(F32), tile(F32), tile(BF16), tile(BF16), tile(BF16), tile(BF16), tile(BF16),
                   tile(BF16), jax.ShapeDtypeStruct((b, CONV_W - 1, d), F32)],
        scratch_shapes=[pltpu.VMEM((8, d), F32)],
        compiler_params=pltpu.CompilerParams(dimension_semantics=("arbitrary", "arbitrary"),
                                             vmem_limit_bytes=V7X_VMEM_LIMIT),
        name="prompt_inproj",
    )(x, gn, w_in, gq, gk, gmat, cos, sin_signed, conv_w)


def _sample_inproj(x, gn, w_in, gq, gk, gmat, cos, sin_signed, conv_w, state, nb):
    n, d = x.shape
    full = lambda dt: jax.ShapeDtypeStruct((n, d), dt)
    args = (x, gn, w_in, gq, gk, gmat, cos, sin_signed, conv_w, state)
    return pl.pallas_call(
        functools.partial(_sample_inproj_kernel, nb=nb),
        out_shape=[full(F32), full(F32), full(F32), full(BF16), full(BF16), full(BF16),
                   jax.ShapeDtypeStruct(((CONV_W - 1) * nb, d), F32)],
        compiler_params=pltpu.CompilerParams(vmem_limit_bytes=V7X_VMEM_LIMIT),
        name="sample_inproj",
    )(*args)


def _prompt_attn(qb, kb, vb, lams, gsub_col, lam_init, tq):
    b, t, d = qb.shape
    q_spec = pl.BlockSpec((1, tq, d), lambda bi, qi: (bi, qi, 0))
    kv_spec = pl.BlockSpec((1, t, d), lambda bi, qi: (bi, 0, 0))
    small = [_resident(a.shape) for a in lams] + [_resident(gsub_col.shape)]
    return pl.pallas_call(
        functools.partial(_prompt_attn_kernel, tq=tq, lam_init=lam_init),
        grid=(b, t // tq),
        in_specs=[q_spec, kv_spec, kv_spec] + small,
        out_specs=q_spec,
        out_shape=jax.ShapeDtypeStruct((b, t, d), BF16),
        scratch_shapes=[pltpu.VMEM((N_HEADS, 2 * tq, HEAD_W), BF16),
                        pltpu.VMEM((N_HEADS, D_V, 2 * tq), F32),
                        pltpu.VMEM((N_HEADS, 1, 2 * tq), F32), pltpu.VMEM((N_HEADS, 1, 2 * tq), F32)],
        compiler_params=pltpu.CompilerParams(dimension_semantics=("arbitrary", "arbitrary"),
                                             vmem_limit_bytes=V7X_VMEM_LIMIT),
        name="prompt_attn",
    )(qb, kb, vb, *lams, gsub_col)


def _decode_biases(n_rows, page_size, n_heads):
    row_head = (jnp.arange(n_rows) % (n_rows // 2)) // (n_rows // 2 // n_heads)
    row_tok = jnp.arange(n_rows) % (n_rows // 2 // n_heads)
    col_head = jnp.arange(page_size * n_heads) % n_heads
    neg = jnp.asarray(-jnp.inf, F32)
    bias = jnp.where(row_head[:, None] == col_head[None, :], 0.0, neg).astype(F32)
    coln = jnp.arange(NEW_KEY_PAD * n_heads)
    biasn = jnp.where((row_head[:, None] == (coln % n_heads)[None, :])
                      & ((coln // n_heads)[None, :] <= row_tok[:, None]), 0.0, neg).astype(F32)
    return bias, biasn


def _out_mlp_decode(an, cin, sga, sgc, x, weights, tm, page_table, qz, kn, vn, cache_k, cache_v, layer,
                    lams, gsub_row, lam_init):
    n, d = x.shape
    nb, n_rows, _ = qz.shape
    page_size, n_heads, hw = cache_k.shape[2:]
    n_tiles = n // tm
    n_pages_step = math.gcd(page_table.shape[1], DECODE_PAGES_PER_STEP)
    nc = page_table.shape[1] // n_pages_step
    n_sub, rem = divmod(nb * nc, n_tiles)
    assert rem == 0 and n_sub >= 1 and nc % n_sub == 0, (nb, nc, n_tiles)
    bias, biasn = _decode_biases(n_rows, page_size, n_heads)

    row_spec = pl.BlockSpec((tm, d), lambda i, pt: (i, 0))
    per_b = lambda a: pl.BlockSpec((1,) + a.shape[1:],
                                   lambda i, pt: (i * n_sub // nc,) + (0,) * (a.ndim - 1))
    consts = (bias, biasn, *lams, gsub_row)
    grid_spec = pltpu.PrefetchScalarGridSpec(
        num_scalar_prefetch=1,
        grid=(n_tiles,),
        in_specs=[row_spec] * 5 + [_resident(w.shape) for w in weights]
                 + [per_b(qz), per_b(kn), per_b(vn)] + [_resident(a.shape) for a in consts]
                 + [pl.BlockSpec(memory_space=pl.ANY)] * 2,
        out_specs=[row_spec, pl.BlockSpec((1, n_rows // 2, hw), lambda i, pt: (i * n_sub // nc, 0, 0))],
        scratch_shapes=[pltpu.VMEM((tm, d), BF16),
                        pltpu.VMEM((DECODE_SLOTS, n_pages_step, page_size, n_heads, hw), cache_k.dtype),
                        pltpu.VMEM((DECODE_SLOTS, n_pages_step, page_size, n_heads, hw), cache_v.dtype),
                        pltpu.SemaphoreType.DMA((2, DECODE_SLOTS)),
                        pltpu.VMEM((n_rows, 1), F32), pltpu.VMEM((n_rows, 1), F32),
                        pltpu.VMEM((n_rows, hw), F32)],
    )
    return pl.pallas_call(
        functools.partial(_out_mlp_decode_kernel, n_sub=n_sub, nc=nc, layer=layer, lam_init=lam_init),
        grid_spec=grid_spec,
        out_shape=[jax.ShapeDtypeStruct((n, d), F32), jax.ShapeDtypeStruct((nb, n_rows // 2, hw), F32)],
        compiler_params=pltpu.CompilerParams(dimension_semantics=("arbitrary",),
                                             vmem_limit_bytes=V7X_VMEM_LIMIT),
        name="out_mlp_decode",
    )(page_table.reshape(-1), an, cin, sga, sgc, x, *weights, qz, kn, vn, *consts, cache_k, cache_v)


def _out_mlp(an, cin, sga, sgc, x, weights, tm):
    n, d = x.shape
    row_spec = pl.BlockSpec((tm, d), lambda i: (i, 0))
    return pl.pallas_call(
        _out_mlp_kernel,
        grid=(n // tm,),
        in_specs=[row_spec] * 5 + [_resident(w.shape) for w in weights],
        out_specs=row_spec,
        out_shape=jax.ShapeDtypeStruct((n, d), F32),
        scratch_shapes=[pltpu.VMEM((tm, d), BF16)],
        compiler_params=pltpu.CompilerParams(dimension_semantics=("arbitrary",),
                                             vmem_limit_bytes=V7X_VMEM_LIMIT),
        name="out_mlp",
    )(an, cin, sga, sgc, x, *weights)


def _layer(l, yp, ys, cache_k, cache_v, state_conv, page_table, p):
    bp, tp, d = yp.shape
    bs, ts, _ = ys.shape
    past = page_table.shape[1] * cache_k.shape[2]
    lam_init = 0.8 - 0.6 * math.exp(-0.3 * l)
    row = lambda a: a.reshape(1, -1).astype(F32)

    gn, gm = row(p["g_attn_norm"][l]), row(p["g_mlp_norm"][l])
    gq = row(jnp.tile(p["g_q"][l], 2 * N_HEADS))
    gk = row(jnp.tile(p["g_k"][l], 2 * N_HEADS))
    gsub_row = row(p["g_sub"][l])
    gsub_col = p["g_sub"][l].reshape(-1, 1).astype(F32)
    lams = [row(p[n][l]) for n in ("lambda_q1", "lambda_k1", "lambda_q2", "lambda_k2")]
    w_in, wa, wc, wo, wup, wdn = (p[n][l].astype(BF16) for n in
                                  ("w_in", "w_attn_out", "w_conv_out", "w_o", "w_up", "w_down"))
    conv_w = p["conv_w"][l].astype(F32)
    gmat = _group_sum_matrix()

    cos_p, sin_p = _rope_tables(jnp.arange(tp))
    tm = _row_tile(tp, INPROJ_ROWS)
    kf, vf, qb, kb, vb, cin, sga, sgc, cst_p = _prompt_inproj(
        yp, gn, w_in, gq, gk, gmat, cos_p, sin_p, conv_w, tm)
    an = _prompt_attn(qb, kb, vb, lams, gsub_col, lam_init, _row_tile(tp, ATTN_ROWS))
    flat = lambda a: a.reshape(bp * tp, d)
    out_weights = (wa, wc, wo, gm, wup, wdn)

    cos_s, sin_s = _rope_tables(past + jnp.arange(ts))
    cos_s, sin_s = jnp.repeat(cos_s, bs, axis=0), jnp.repeat(sin_s, bs, axis=0)
    x_tb = jnp.swapaxes(ys, 0, 1).reshape(ts * bs, d)
    st_tb = jnp.swapaxes(state_conv[l].astype(F32), 0, 1).reshape((CONV_W - 1) * bs, d)
    q_s, k_s, v_s, cin_s, sga_s, sgc_s, cst_s = _sample_inproj(
        x_tb, gn, w_in, gq, gk, gmat, cos_s, sin_s, conv_w, st_tb, bs)
    to_bt = lambda a: jnp.swapaxes(a.reshape(ts, bs, N_HEADS, HEAD_W), 0, 1)
    k_new, v_new = to_bt(k_s), to_bt(v_s)
    q5 = jnp.transpose(q_s.reshape(ts, bs, N_HEADS, 2, D_QK), (1, 3, 2, 0, 4))
    q5 = q5.reshape(bs, 2, N_HEADS * ts, D_QK)
    zeros = jnp.zeros_like(q5[:, 0])
    qz = jnp.stack([jnp.concatenate([q5[:, 0], zeros], axis=-1),
                    jnp.concatenate([zeros, q5[:, 1]], axis=-1)], axis=1)
    qz = qz.reshape(bs, 2 * N_HEADS * ts, HEAD_W)
    pad = ((0, 0), (0, NEW_KEY_PAD - ts), (0, 0), (0, 0))
    yp_new, o_s = _out_mlp_decode(
        flat(an), flat(cin), flat(sga), flat(sgc), flat(yp), out_weights, _row_tile(bp * tp, OUT_MLP_ROWS),
        page_table, qz, jnp.pad(k_new, pad), jnp.pad(v_new, pad), cache_k, cache_v, l,
        lams, gsub_row, lam_init)
    yp_new = yp_new.reshape(bp, tp, d)
    an_s = jnp.transpose(o_s.reshape(bs, N_HEADS, ts, D_V), (2, 0, 1, 3)).reshape(ts * bs, d)
    ys_tb = _out_mlp(an_s.astype(BF16), cin_s, sga_s, sgc_s, x_tb, out_weights,
                     _row_tile(ts * bs, OUT_MLP_ROWS))
    ys_new = jnp.swapaxes(ys_tb.reshape(ts, bs, d), 0, 1)
    cst_s = jnp.swapaxes(cst_s.reshape(CONV_W - 1, bs, d), 0, 1)

    heads = lambda a: a.reshape(bp, tp, N_HEADS, HEAD_W)
    return yp_new, ys_new, (heads(kf), heads(vf), cst_p, k_new, v_new, cst_s)


def kernel(x_prompt, x_sample, cache_k, cache_v, state_conv, page_table, g_attn_norm, w_in, g_q, g_k,
           lambda_q1, lambda_k1, lambda_q2, lambda_k2, g_sub, w_attn_out, conv_w, w_conv_out, w_o,
           g_mlp_norm, w_up, w_down):
    params = dict(g_attn_norm=g_attn_norm, w_in=w_in, g_q=g_q, g_k=g_k, lambda_q1=lambda_q1,
                  lambda_k1=lambda_k1, lambda_q2=lambda_q2, lambda_k2=lambda_k2, g_sub=g_sub,
                  w_attn_out=w_attn_out, conv_w=conv_w, w_conv_out=w_conv_out, w_o=w_o,
                  g_mlp_norm=g_mlp_norm, w_up=w_up, w_down=w_down)
    yp, ys = x_prompt, x_sample
    caches = []
    for l in range(w_in.shape[0]):
        yp, ys, new = _layer(l, yp, ys, cache_k, cache_v, state_conv, page_table, params)
        caches.append(new)
    stacked = [jnp.stack([c[i] for c in caches]) for i in range(6)]
    return (yp, ys, *stacked)
```

```python
import functools
import math

import jax
import jax.numpy as jnp
from jax import lax
from jax.experimental import pallas as pl
from jax.experimental.pallas import tpu as pltpu

F32 = jnp.float32
BF16 = jnp.bfloat16

N_HEADS = 8
D_QK = 64
D_V = 2 * D_QK
HEAD_W = 2 * D_QK
CONV_W = 3
ROPE_THETA = 10000.0
EPS = 1e-6
SCALE = 1.0 / math.sqrt(D_QK)
LOG2_E = math.log2(math.e)
N_IN_CHUNKS = 8
NEW_KEY_PAD = 16

V7X_VMEM_LIMIT = 56 * 1024 * 1024
INPROJ_ROWS = 512
OUT_MLP_ROWS = 256
ATTN_ROWS = 256
DECODE_PAGES_PER_STEP = 8
DECODE_SLOTS = 2
DECODE_GROUP_PAGES = 8
N_MLP_INPUTS = 11
N_DECODE_INPUTS = 12
GROUP_TILE = 256

NT_DIMS = (((1,), (1,)), ((), ()))
TN_DIMS = (((0,), (0,)), ((), ()))


def _dot(a, b):
    return jnp.dot(a, b, preferred_element_type=F32)


def _rms_rows(x, g_row):
    return x * lax.rsqrt(jnp.mean(x * x, axis=-1, keepdims=True) + EPS) * g_row


def _lambda(lq1_ref, lk1_ref, lq2_ref, lk2_ref, lam_init):
    s1 = jnp.sum(lq1_ref[...] * lk1_ref[...], axis=-1, keepdims=True)
    s2 = jnp.sum(lq2_ref[...] * lk2_ref[...], axis=-1, keepdims=True)
    return jnp.exp(s1) - jnp.exp(s2) + lam_init


def _qk_norm_rope(z, g_row, gmat, cos, sin_signed, upper_half):
    zz = (z * z).astype(BF16)
    n_tiles = z.shape[1] // GROUP_TILE
    ss = jnp.concatenate(
        [_dot(zz[:, c * GROUP_TILE:(c + 1) * GROUP_TILE], gmat) for c in range(n_tiles)], axis=1)
    y = z * lax.rsqrt(ss * (1.0 / D_QK) + EPS) * g_row
    outs = []
    for h in range(N_HEADS):
        ys = y[:, h * HEAD_W:(h + 1) * HEAD_W]
        partner = jnp.where(upper_half, pltpu.roll(ys, D_QK // 2, 1),
                            pltpu.roll(ys, HEAD_W - D_QK // 2, 1))
        outs.append(ys * cos + partner * sin_signed)
    return jnp.concatenate(outs, axis=1)


def _inproj_common(x, gn_ref, w_ref, gq_ref, gk_ref, gmat_ref, cos_ref, sin_ref):
    rows, d = x.shape
    rs = lax.rsqrt(jnp.mean(x * x, axis=-1, keepdims=True) + EPS)
    xg = (x * gn_ref[...]).astype(BF16)
    zc = lambda c: _dot(xg, w_ref[:, c * d:(c + 1) * d]) * rs
    cos, sin_signed = cos_ref[...], sin_ref[...]
    upper_half = (lax.broadcasted_iota(jnp.int32, (rows, HEAD_W), 1) & (D_QK // 2)) != 0
    gmat = gmat_ref[...]
    q = _qk_norm_rope(zc(0), gq_ref[...], gmat, cos, sin_signed, upper_half)
    k = _qk_norm_rope(zc(1), gk_ref[...], gmat, cos, sin_signed, upper_half)
    u = zc(3) * zc(5)
    u_b = zc(4)
    sga = jax.nn.sigmoid(zc(6))
    sgc = jax.nn.sigmoid(zc(7))
    v = zc(2)
    return q, k, v, u, u_b, sga, sgc


def _prompt_inproj_kernel(x_ref, gn_ref, w_ref, gq_ref, gk_ref, gmat_ref, cos_ref, sin_ref, cw_ref,
                          kf_ref, vf_ref, qb_ref, kb_ref, vb_ref, cin_ref, sga_ref, sgc_ref, cst_ref,
                          carry_ref):
    @pl.when(pl.program_id(1) == 0)
    def _():
        carry_ref[...] = jnp.zeros_like(carry_ref)

    x = x_ref[0]
    tm = x.shape[0]
    q, k, v, u, u_b, sga, sgc = _inproj_common(x, gn_ref, w_ref, gq_ref, gk_ref, gmat_ref,
                                               cos_ref, sin_ref)
    qb_ref[0] = (q * (SCALE * LOG2_E)).astype(BF16)
    kf_ref[0] = k
    kb_ref[0] = k.astype(BF16)

    prev = carry_ref[...]
    c0, c1 = prev[6:7], prev[7:8]
    rows = lax.broadcasted_iota(jnp.int32, (tm, 1), 0)
    u1 = jnp.where(rows == 0, c1, pltpu.roll(u, 1, 0))
    u2 = jnp.where(rows == 0, c0, jnp.where(rows == 1, c1, pltpu.roll(u, 2, 0)))
    cw = cw_ref[...]
    cy = cw[0:1] * u2 + cw[1:2] * u1 + cw[2:3] * u
    carry_ref[...] = u[tm - 8:tm]
    cst_ref[0] = u[tm - (CONV_W - 1):tm]
    cin_ref[0] = (u_b * cy).astype(BF16)
    sga_ref[0] = sga.astype(BF16)
    sgc_ref[0] = sgc.astype(BF16)
    vf_ref[0] = v
    vb_ref[0] = v.astype(BF16)


def _sample_inproj_kernel(x_ref, gn_ref, w_ref, gq_ref, gk_ref, gmat_ref, cos_ref, sin_ref, cw_ref,
                          st_ref, qf_ref, kf_ref, vf_ref, cin_ref, sga_ref, sgc_ref, cst_ref, *, nb):
    x = x_ref[...]
    n = x.shape[0]
    q, k, v, u, u_b, sga, sgc = _inproj_common(x, gn_ref, w_ref, gq_ref, gk_ref, gmat_ref,
                                               cos_ref, sin_ref)
    qf_ref[...] = q
    kf_ref[...] = k
    vf_ref[...] = v
    ue = jnp.concatenate([st_ref[...], u], axis=0)
    cw = cw_ref[...]
    cy = cw[0:1] * ue[0:n] + cw[1:2] * ue[nb:nb + n] + cw[2:3] * ue[2 * nb:2 * nb + n]
    cst_ref[...] = ue[n:n + (CONV_W - 1) * nb]
    cin_ref[...] = (u_b * cy).astype(BF16)
    sga_ref[...] = sga.astype(BF16)
    sgc_ref[...] = sgc.astype(BF16)


def _prompt_attn_kernel(q_ref, k_ref, v_ref, lq1_ref, lk1_ref, lq2_ref, lk2_ref, gsub_ref,
                        o_ref, qs_ref, acc_ref, m_ref, l_ref, *, tq, lam_init):
    qi = pl.program_id(1)
    lam = _lambda(lq1_ref, lk1_ref, lq2_ref, lk2_ref, lam_init)
    lane = lax.broadcasted_iota(jnp.int32, (tq, HEAD_W), 1)
    key_row = lax.broadcasted_iota(jnp.int32, (tq, 2 * tq), 0)
    q_col = lax.broadcasted_iota(jnp.int32, (tq, 2 * tq), 1)
    causal = key_row <= jnp.where(q_col >= tq, q_col - tq, q_col)
    head_lanes = [slice(h * HEAD_W, (h + 1) * HEAD_W) for h in range(N_HEADS)]

    for h, hs in enumerate(head_lanes):
        qh = q_ref[0, :, hs]
        zero = jnp.zeros_like(qh)
        qs_ref[h] = jnp.concatenate(
            [jnp.where(lane < D_QK, qh, zero), jnp.where(lane >= D_QK, qh, zero)], axis=0)
    m_ref[...] = jnp.full_like(m_ref, -jnp.inf)
    l_ref[...] = jnp.zeros_like(l_ref)
    acc_ref[...] = jnp.zeros_like(acc_ref)

    def step(kt, masked):
        ks = pl.ds(pl.multiple_of(kt * tq, tq), tq)
        scores = [lax.dot_general(k_ref[0, ks, hs], qs_ref[h], NT_DIMS, preferred_element_type=F32)
                  for h, hs in enumerate(head_lanes)]
        for h, hs in enumerate(head_lanes):
            s = scores[h]
            if masked:
                s = jnp.where(causal, s, -jnp.inf)
            m_prev = m_ref[h]
            m_new = jnp.maximum(m_prev, jnp.max(s, axis=0, keepdims=True))
            corr = jnp.exp2(m_prev - m_new)
            p = jnp.exp2(s - m_new)
            l_ref[h] = l_ref[h] * corr + jnp.sum(p, axis=0, keepdims=True)
            pv = lax.dot_general(v_ref[0, ks, hs], p.astype(BF16), TN_DIMS,
                                 preferred_element_type=F32)
            acc_ref[h] = acc_ref[h] * corr + pv
            m_ref[h] = m_new

    def body(kt, carry):
        step(kt, False)
        return carry

    lax.fori_loop(0, qi, body, 0)
    step(qi, True)

    for h, hs in enumerate(head_lanes):
        o = acc_ref[h] * (1.0 / l_ref[h])
        od = o[:, :tq] - lam * o[:, tq:]
        ms = jnp.mean(od * od, axis=0, keepdims=True)
        on = od * lax.rsqrt(ms + EPS) * gsub_ref[...] * (1.0 - lam_init)
        o_ref[0, :, hs] = on.T.astype(BF16)


def _decode_step(step, total, nc, pt_ref, qz_ref, kn_ref, vn_ref, bias_ref, biasn_ref,
                 lq1_ref, lk1_ref, lq2_ref, lk2_ref, gsub_ref, k_hbm, v_hbm,
                 o_ref, kbuf, vbuf, sem, m_ref, l_ref, acc_ref, *, layer, lam_init, overlap=None):
    n_slots, n_pages_step = kbuf.shape[:2]
    c = step % nc
    n_rows = qz_ref.shape[1]

    def page_copies(s, slot):
        copies = []
        for j in range(n_pages_step):
            page = pt_ref[s * n_pages_step + j]
            copies.append(pltpu.make_async_copy(k_hbm.at[layer, page], kbuf.at[slot, j], sem.at[0, slot]))
            copies.append(pltpu.make_async_copy(v_hbm.at[layer, page], vbuf.at[slot, j], sem.at[1, slot]))
        return copies

    @pl.when(step == 0)
    def _():
        for s in range(n_slots - 1):
            @pl.when(s < total)
            def _():
                for cp in page_copies(s, s):
                    cp.start()

    ahead = step + (n_slots - 1)

    @pl.when(ahead < total)
    def _():
        for cp in page_copies(ahead, ahead % n_slots):
            cp.start()

    @pl.when(c == 0)
    def _():
        m_ref[...] = jnp.full_like(m_ref, -jnp.inf)
        l_ref[...] = jnp.zeros_like(l_ref)
        acc_ref[...] = jnp.zeros_like(acc_ref)

    if overlap is not None:
        overlap()

    slot = step % n_slots
    for cp in page_copies(step, slot):
        cp.wait()

    qz = (qz_ref[0] * (SCALE * LOG2_E)).astype(BF16)

    def scores(k3, bias):
        kp = k3.reshape(k3.shape[0] * k3.shape[1], k3.shape[2]).astype(BF16)
        return lax.dot_general(qz, kp, NT_DIMS, preferred_element_type=F32) + bias

    def update(s_list, v_list):
        m_prev = m_ref[...]
        m_new = m_prev
        for s in s_list:
            m_new = jnp.maximum(m_new, jnp.max(s, axis=1, keepdims=True))
        corr = jnp.exp2(m_prev - m_new)
        l_new = l_ref[...] * corr
        acc = acc_ref[...] * corr
        for s, v3 in zip(s_list, v_list):
            p = jnp.exp2(s - m_new)
            l_new = l_new + jnp.sum(p, axis=1, keepdims=True)
            vp = v3.reshape(v3.shape[0] * v3.shape[1], v3.shape[2]).astype(BF16)
            acc = acc + _dot(p.astype(BF16), vp)
        m_ref[...] = m_new
        l_ref[...] = l_new
        acc_ref[...] = acc

    bias = bias_ref[...]
    group = math.gcd(n_pages_step, DECODE_GROUP_PAGES)
    s_all = [scores(kbuf[slot, j], bias) for j in range(n_pages_step)]
    for g in range(0, n_pages_step, group):
        update(s_all[g:g + group], [vbuf[slot, j] for j in range(g, g + group)])

    @pl.when(c == nc - 1)
    def _():
        update([scores(kn_ref[0], biasn_ref[...])], [vn_ref[0]])
        lam = _lambda(lq1_ref, lk1_ref, lq2_ref, lk2_ref, lam_init)
        o = acc_ref[...] * (1.0 / l_ref[...])
        od = o[:n_rows // 2] - lam * o[n_rows // 2:]
        o_ref[0] = _rms_rows(od, gsub_ref[...]) * (1.0 - lam_init)


def _out_mlp_phases(an_ref, cin_ref, sga_ref, sgc_ref, x_ref, wa_ref, wc_ref, wo_ref, gm_ref,
                    wup_ref, wdn_ref, y_ref, xn_ref):
    d = x_ref.shape[1]

    def mix():
        a = _dot(an_ref[...], wa_ref[...])
        cv = _dot(cin_ref[...], wc_ref[...])
        h = sga_ref[...].astype(F32) * a + sgc_ref[...].astype(F32) * cv
        y = x_ref[...] + _dot(h.astype(BF16), wo_ref[...])
        xn_ref[...] = _rms_rows(y, gm_ref[...]).astype(BF16)
        y_ref[...] = y

    w = d

    def mlp_chunk(c):
        def run():
            hc = _dot(xn_ref[...], wup_ref[:, c * w:(c + 1) * w])
            hc = jnp.square(jnp.maximum(hc, 0.0)).astype(BF16)
            y_ref[...] += _dot(hc, wdn_ref[c * w:(c + 1) * w, :])
        return run

    return [(mix, 3.0)] + [(mlp_chunk(c), 2.0) for c in range(wup_ref.shape[1] // w)]


def _out_mlp_kernel(*refs):
    for phase, _ in _out_mlp_phases(*refs):
        phase()


def _out_mlp_decode_kernel(pt_ref, *refs, n_sub, nc, layer, lam_init):
    n_in = N_MLP_INPUTS + N_DECODE_INPUTS
    mlp_in, dec_in = refs[:N_MLP_INPUTS], refs[N_MLP_INPUTS:n_in]
    y_ref, o_ref, xn_ref = refs[n_in:n_in + 3]
    dec_refs = dec_in + (o_ref,) + refs[n_in + 3:]
    phases = _out_mlp_phases(*mlp_in, y_ref, xn_ref)
    total = pl.num_programs(0) * n_sub
    total_cost = sum(cost for _, cost in phases)
    owner, done = [], 0.0
    for _, cost in phases:
        owner.append(min(int((done + cost / 2) * n_sub / total_cost), n_sub - 1))
        done += cost

    for k in range(n_sub):
        def run_phases(k=k):
            for (phase, _), own in zip(phases, owner):
                if own == k:
                    phase()

        _decode_step(pl.program_id(0) * n_sub + k, total, nc, pt_ref, *dec_refs,
                     layer=layer, lam_init=lam_init, overlap=run_phases)


def _resident(shape):
    nd = len(shape)
    return pl.BlockSpec(shape, lambda *_: (0,) * nd, pipeline_mode=pl.Buffered(1))


def _rope_tables(pos):
    inv = ROPE_THETA ** (-jnp.arange(0, D_QK, 2, dtype=F32) / D_QK)
    ang = pos.astype(F32)[:, None] * inv[None, :]
    cos, sin = jnp.cos(ang), jnp.sin(ang)
    return jnp.tile(cos, (1, 4)), jnp.tile(jnp.concatenate([-sin, sin], axis=1), (1, 2))


def _group_sum_matrix():
    r = jnp.arange(GROUP_TILE) // D_QK
    return (r[:, None] == r[None, :]).astype(BF16)


def _row_tile(n, target):
    t = min(n, target)
    while n % t:
        t //= 2
    return t


def _prompt_inproj(x, gn, w_in, gq, gk, gmat, cos, sin_signed, conv_w, tm):
    b, t, d = x.shape
    tile = lambda dt: jax.ShapeDtypeStruct((b, t, d), dt)
    row_spec = pl.BlockSpec((1, tm, d), lambda bi, i: (bi, i, 0))
    tab_spec = pl.BlockSpec((tm, HEAD_W), lambda bi, i: (i, 0))
    return pl.pallas_call(
        _prompt_inproj_kernel,
        grid=(b, t // tm),
        in_specs=[row_spec, _resident(gn.shape), _resident(w_in.shape), _resident(gq.shape),
                  _resident(gk.shape), _resident(gmat.shape), tab_spec, tab_spec,
                  _resident(conv_w.shape)],
        out_specs=[row_spec] * 8 + [pl.BlockSpec((1, CONV_W - 1, d), lambda bi, i: (bi, 0, 0))],
        out_shape=[tile(F32), tile(F32), tile(BF16), tile(BF16), tile(BF16), tile(BF16), tile(BF16),
                   tile(BF16), jax.ShapeDtypeStruct((b, CONV_W - 1, d), F32)],
        scratch_shapes=[pltpu.VMEM((8, d), F32)],
        compiler_params=pltpu.CompilerParams(dimension_semantics=("arbitrary", "arbitrary"),
                                             vmem_limit_bytes=V7X_VMEM_LIMIT),
        name="prompt_inproj",
    )(x, gn, w_in, gq, gk, gmat, cos, sin_signed, conv_w)


def _sample_inproj(x, gn, w_in, gq, gk, gmat, cos, sin_signed, conv_w, state, nb):
    n, d = x.shape
    full = lambda dt: jax.ShapeDtypeStruct((n, d), dt)
    args = (x, gn, w_in, gq, gk, gmat, cos, sin_signed, conv_w, state)
    return pl.pallas_call(
        functools.partial(_sample_inproj_kernel, nb=nb),
        out_shape=[full(F32), full(F32), full(F32), full(BF16), full(BF16), full(BF16),
                   jax.ShapeDtypeStruct(((CONV_W - 1) * nb, d), F32)],
        compiler_params=pltpu.CompilerParams(vmem_limit_bytes=V7X_VMEM_LIMIT),
        name="sample_inproj",
    )(*args)


def _prompt_attn(qb, kb, vb, lams, gsub_col, lam_init, tq):
    b, t, d = qb.shape
    q_spec = pl.BlockSpec((1, tq, d), lambda bi, qi: (bi, qi, 0))
    kv_spec = pl.BlockSpec((1, t, d), lambda bi, qi: (bi, 0, 0))
    small = [_resident(a.shape) for a in lams] + [_resident(gsub_col.shape)]
    return pl.pallas_call(
        functools.partial(_prompt_attn_kernel, tq=tq, lam_init=lam_init),
        grid=(b, t // tq),
        in_specs=[q_spec, kv_spec, kv_spec] + small,
        out_specs=q_spec,
        out_shape=jax.ShapeDtypeStruct((b, t, d), BF16),
        scratch_shapes=[pltpu.VMEM((N_HEADS, 2 * tq, HEAD_W), BF16),
                        pltpu.VMEM((N_HEADS, D_V, 2 * tq), F32),
                        pltpu.VMEM((N_HEADS, 1, 2 * tq), F32), pltpu.VMEM((N_HEADS, 1, 2 * tq), F32)],
        compiler_params=pltpu.CompilerParams(dimension_semantics=("arbitrary", "arbitrary"),
                                             vmem_limit_bytes=V7X_VMEM_LIMIT),
        name="prompt_attn",
    )(qb, kb, vb, *lams, gsub_col)


def _decode_biases(n_rows, page_size, n_heads):
    row_head = (jnp.arange(n_rows) % (n_rows // 2)) // (n_rows // 2 // n_heads)
    row_tok = jnp.arange(n_rows) % (n_rows // 2 // n_heads)
    col_head = jnp.arange(page_size * n_heads) % n_heads
    neg = jnp.asarray(-jnp.inf, F32)
    bias = jnp.where(row_head[:, None] == col_head[None, :], 0.0, neg).astype(F32)
    coln = jnp.arange(NEW_KEY_PAD * n_heads)
    biasn = jnp.where((row_head[:, None] == (coln % n_heads)[None, :])
                      & ((coln // n_heads)[None, :] <= row_tok[:, None]), 0.0, neg).astype(F32)
    return bias, biasn


def _out_mlp_decode(an, cin, sga, sgc, x, weights, tm, page_table, qz, kn, vn, cache_k, cache_v, layer,
                    lams, gsub_row, lam_init):
    n, d = x.shape
    nb, n_rows, _ = qz.shape
    page_size, n_heads, hw = cache_k.shape[2:]
    n_tiles = n // tm
    n_pages_step = math.gcd(page_table.shape[1], DECODE_PAGES_PER_STEP)
    nc = page_table.shape[1] // n_pages_step
    n_sub, rem = divmod(nb * nc, n_tiles)
    assert rem == 0 and n_sub >= 1 and nc % n_sub == 0, (nb, nc, n_tiles)
    bias, biasn = _decode_biases(n_rows, page_size, n_heads)

    row_spec = pl.BlockSpec((tm, d), lambda i, pt: (i, 0))
    per_b = lambda a: pl.BlockSpec((1,) + a.shape[1:],
                                   lambda i, pt: (i * n_sub // nc,) + (0,) * (a.ndim - 1))
    consts = (bias, biasn, *lams, gsub_row)
    grid_spec = pltpu.PrefetchScalarGridSpec(
        num_scalar_prefetch=1,
        grid=(n_tiles,),
        in_specs=[row_spec] * 5 + [_resident(w.shape) for w in weights]
                 + [per_b(qz), per_b(kn), per_b(vn)] + [_resident(a.shape) for a in consts]
                 + [pl.BlockSpec(memory_space=pl.ANY)] * 2,
        out_specs=[row_spec, pl.BlockSpec((1, n_rows // 2, hw), lambda i, pt: (i * n_sub // nc, 0, 0))],
        scratch_shapes=[pltpu.VMEM((tm, d), BF16),
                        pltpu.VMEM((DECODE_SLOTS, n_pages_step, page_size, n_heads, hw), cache_k.dtype),
                        pltpu.VMEM((DECODE_SLOTS, n_pages_step, page_size, n_heads, hw), cache_v.dtype),
                        pltpu.SemaphoreType.DMA((2, DECODE_SLOTS)),
                        pltpu.VMEM((n_rows, 1), F32), pltpu.VMEM((n_rows, 1), F32),
                        pltpu.VMEM((n_rows, hw), F32)],
    )
    return pl.pallas_call(
        functools.partial(_out_mlp_decode_kernel, n_sub=n_sub, nc=nc, layer=layer, lam_init=lam_init),
        grid_spec=grid_spec,
        out_shape=[jax.ShapeDtypeStruct((n, d), F32), jax.ShapeDtypeStruct((nb, n_rows // 2, hw), F32)],
        compiler_params=pltpu.CompilerParams(dimension_semantics=("arbitrary",),
                                             vmem_limit_bytes=V7X_VMEM_LIMIT),
        name="out_mlp_decode",
    )(page_table.reshape(-1), an, cin, sga, sgc, x, *weights, qz, kn, vn, *consts, cache_k, cache_v)


def _out_mlp(an, cin, sga, sgc, x, weights, tm):
    n, d = x.shape
    row_spec = pl.BlockSpec((tm, d), lambda i: (i, 0))
    return pl.pallas_call(
        _out_mlp_kernel,
        grid=(n // tm,),
        in_specs=[row_spec] * 5 + [_resident(w.shape) for w in weights],
        out_specs=row_spec,
        out_shape=jax.ShapeDtypeStruct((n, d), F32),
        scratch_shapes=[pltpu.VMEM((tm, d), BF16)],
        compiler_params=pltpu.CompilerParams(dimension_semantics=("arbitrary",),
                                             vmem_limit_bytes=V7X_VMEM_LIMIT),
        name="out_mlp",
    )(an, cin, sga, sgc, x, *weights)


def _layer(l, yp, ys, cache_k, cache_v, state_conv, page_table, p):
    bp, tp, d = yp.shape
    bs, ts, _ = ys.shape
    past = page_table.shape[1] * cache_k.shape[2]
    lam_init = 0.8 - 0.6 * math.exp(-0.3 * l)
    row = lambda a: a.reshape(1, -1).astype(F32)

    gn, gm = row(p["g_attn_norm"][l]), row(p["g_mlp_norm"][l])
    gq = row(jnp.tile(p["g_q"][l], 2 * N_HEADS))
    gk = row(jnp.tile(p["g_k"][l], 2 * N_HEADS))
    gsub_row = row(p["g_sub"][l])
    gsub_col = p["g_sub"][l].reshape(-1, 1).astype(F32)
    lams = [row(p[n][l]) for n in ("lambda_q1", "lambda_k1", "lambda_q2", "lambda_k2")]
    w_in, wa, wc, wo, wup, wdn = (p[n][l].astype(BF16) for n in
                                  ("w_in", "w_attn_out", "w_conv_out", "w_o", "w_up", "w_down"))
    conv_w = p["conv_w"][l].astype(F32)
    gmat = _group_sum_matrix()

    cos_p, sin_p = _rope_tables(jnp.arange(tp))
    tm = _row_tile(tp, INPROJ_ROWS)
    kf, vf, qb, kb, vb, cin, sga, sgc, cst_p = _prompt_inproj(
        yp, gn, w_in, gq, gk, gmat, cos_p, sin_p, conv_w, tm)
    an = _prompt_attn(qb, kb, vb, lams, gsub_col, lam_init, _row_tile(tp, ATTN_ROWS))
    flat = lambda a: a.reshape(bp * tp, d)
    out_weights = (wa, wc, wo, gm, wup, wdn)

    cos_s, sin_s = _rope_tables(past + jnp.arange(ts))
    cos_s, sin_s = jnp.repeat(cos_s, bs, axis=0), jnp.repeat(sin_s, bs, axis=0)
    x_tb = jnp.swapaxes(ys, 0, 1).reshape(ts * bs, d)
    st_tb = jnp.swapaxes(state_conv[l].astype(F32), 0, 1).reshape((CONV_W - 1) * bs, d)
    q_s, k_s, v_s, cin_s, sga_s, sgc_s, cst_s = _sample_inproj(
        x_tb, gn, w_in, gq, gk, gmat, cos_s, sin_s, conv_w, st_tb, bs)
    to_bt = lambda a: jnp.swapaxes(a.reshape(ts, bs, N_HEADS, HEAD_W), 0, 1)
    k_new, v_new = to_bt(k_s), to_bt(v_s)
    q5 = jnp.transpose(q_s.reshape(ts, bs, N_HEADS, 2, D_QK), (1, 3, 2, 0, 4))
    q5 = q5.reshape(bs, 2, N_HEADS * ts, D_QK)
    zeros = jnp.zeros_like(q5[:, 0])
    qz = jnp.stack([jnp.concatenate([q5[:, 0], zeros], axis=-1),
                    jnp.concatenate([zeros, q5[:, 1]], axis=-1)], axis=1)
    qz = qz.reshape(bs, 2 * N_HEADS * ts, HEAD_W)
    pad = ((0, 0), (0, NEW_KEY_PAD - ts), (0, 0), (0, 0))
    yp_new, o_s = _out_mlp_decode(
        flat(an), flat(cin), flat(sga), flat(sgc), flat(yp), out_weights, _row_tile(bp * tp, OUT_MLP_ROWS),
        page_table, qz, jnp.pad(k_new, pad), jnp.pad(v_new, pad), cache_k, cache_v, l,
        lams, gsub_row, lam_init)
    yp_new = yp_new.reshape(bp, tp, d)
    an_s = jnp.transpose(o_s.reshape(bs, N_HEADS, ts, D_V), (2, 0, 1, 3)).reshape(ts * bs, d)
    ys_tb = _out_mlp(an_s.astype(BF16), cin_s, sga_s, sgc_s, x_tb, out_weights,
                     _row_tile(ts * bs, OUT_MLP_ROWS))
    ys_new = jnp.swapaxes(ys_tb.reshape(ts, bs, d), 0, 1)
    cst_s = jnp.swapaxes(cst_s.reshape(CONV_W - 1, bs, d), 0, 1)

    heads = lambda a: a.reshape(bp, tp, N_HEADS, HEAD_W)
    return yp_new, ys_new, (heads(kf), heads(vf), cst_p, k_new, v_new, cst_s)


def kernel(x_prompt, x_sample, cache_k, cache_v, state_conv, page_table, g_attn_norm, w_in, g_q, g_k,
           lambda_q1, lambda_k1, lambda_q2, lambda_k2, g_sub, w_attn_out, conv_w, w_conv_out, w_o,
           g_mlp_norm, w_up, w_down):
    params = dict(g_attn_norm=g_attn_norm, w_in=w_in, g_q=g_q, g_k=g_k, lambda_q1=lambda_q1,
                  lambda_k1=lambda_k1, lambda_q2=lambda_q2, lambda_k2=lambda_k2, g_sub=g_sub,
                  w_attn_out=w_attn_out, conv_w=conv_w, w_conv_out=w_conv_out, w_o=w_o,
                  g_mlp_norm=g_mlp_norm, w_up=w_up, w_down=w_down)
    yp, ys = x_prompt, x_sample
    caches = []
    for l in range(w_in.shape[0]):
        yp, ys, new = _layer(l, yp, ys, cache_k, cache_v, state_conv, page_table, params)
        caches.append(new)
    stacked = [jnp.stack([c[i] for c in caches]) for i in range(6)]
    return (yp, ys, *stacked)
```
